```python
import jax, jax.numpy as jnp
from jax import lax
import numpy as np

D_MODEL = 1024
BATCH = 8
SEQ = 8192
DEPTH = 1

MLSTM_HEADS = 4
MLSTM_HEAD_DIM = 128
MLSTM_WIDTH = MLSTM_HEADS * MLSTM_HEAD_DIM
MLSTM_CHUNK = 128
N_GATES = 4 * MLSTM_HEADS
MLA_HEADS = 4
QK_NOPE = 128
QK_ROPE = 64
V_HEAD = 128
Q_LORA = 256
KV_LORA = 128
MLA_WIDTH = MLA_HEADS * V_HEAD
Q_BLOCK = 128
ROPE_THETA = 10000.0
D_MIX = MLSTM_WIDTH + MLA_WIDTH
OFF_Q = 0
OFF_K = OFF_Q + MLSTM_WIDTH
OFF_V = OFF_K + MLSTM_WIDTH
OFF_O = OFF_V + MLSTM_WIDTH
OFF_G = OFF_O + MLSTM_WIDTH
OFF_CQ = OFF_G + N_GATES
OFF_CKV = OFF_CQ + Q_LORA
OFF_KR = OFF_CKV + KV_LORA
IN_COLS = OFF_KR + QK_ROPE
D_FF = 2816
CONV_WIDTH = 3
PLE_DIM = 256
EPS = 1e-6

kernel_name = 'hybrid_mlstm_mla_convffn_block'


def rms_norm(x, g):
    xf = x.astype(jnp.float32)
    y = xf * lax.rsqrt(jnp.mean(xf * xf, axis=-1, keepdims=True) + EPS)
    return (y * g.astype(jnp.float32)).astype(x.dtype)


def dwconv3_centred(x, w, b):
    xp = jnp.pad(x, ((0, 0), (1, 1), (0, 0)))
    return xp[:, :-2] * w[0] + xp[:, 1:-1] * w[1] + xp[:, 2:] * w[2] + b


def apply_rope(x, cos, sin):
    xf = x.astype(jnp.float32)
    x1, x2 = jnp.split(xf, 2, axis=-1)
    return jnp.concatenate([x1 * cos - x2 * sin, x2 * cos + x1 * sin], axis=-1).astype(x.dtype)


def mlstm_chunkwise(q, k, v, i_pre, f_pre):
    b, nh, s, dh = q.shape
    L = MLSTM_CHUNK
    nc = s // L
    qc = q.reshape(b, nh, nc, L, dh)
    kc = k.reshape(b, nh, nc, L, dh)
    vc = v.reshape(b, nh, nc, L, dh)
    log_f = jax.nn.log_sigmoid(f_pre).reshape(b, nh, nc, L)
    log_i = i_pre.reshape(b, nh, nc, L)
    g = jnp.cumsum(log_f, axis=-1)
    g_last = g[..., -1]
    a = g_last[..., None] - g + log_i
    m_loc = jnp.max(a, axis=-1)
    w = jnp.exp(a - m_loc[..., None])
    c_loc = jnp.einsum('bhcsd,bhcse->bhcde', vc * w[..., None], kc)
    n_loc = jnp.einsum('bhcs,bhcse->bhce', w, kc)

    def step(carry, inp):
        c_st, n_st, m_st = carry
        gl, ml, cl, nl = inp
        m_new = jnp.maximum(gl + m_st, ml)
        s_old = jnp.exp(gl + m_st - m_new)
        s_new = jnp.exp(ml - m_new)
        c_next = s_old[..., None, None] * c_st + s_new[..., None, None] * cl
        n_next = s_old[..., None] * n_st + s_new[..., None] * nl
        return (c_next, n_next, m_new), (c_st, n_st, m_st)

    init = (jnp.zeros((b, nh, dh, dh), q.dtype), jnp.zeros((b, nh, dh), q.dtype),
            jnp.zeros((b, nh), q.dtype))
    xs = (jnp.moveaxis(g_last, 2, 0), jnp.moveaxis(m_loc, 2, 0),
          jnp.moveaxis(c_loc, 2, 0), jnp.moveaxis(n_loc, 2, 0))
    _, (c_prev, n_prev, m_prev) = lax.scan(step, init, xs)
    c_prev = jnp.moveaxis(c_prev, 0, 2)
    n_prev = jnp.moveaxis(n_prev, 0, 2)
    m_prev = jnp.moveaxis(m_prev, 0, 2)

    lower_tri = jnp.tril(jnp.ones((L, L), dtype=bool))
    d = jnp.where(lower_tri, g[..., :, None] - g[..., None, :] + log_i[..., None, :], -jnp.inf)
    b_inter = g + m_prev[..., None]
    m_t = jnp.maximum(jnp.max(d, axis=-1), b_inter)
    scores = jnp.einsum('bhctd,bhcsd->bhcts', qc, kc) * jnp.exp(d - m_t[..., None])
    inter = jnp.exp(b_inter - m_t)
    num = (jnp.einsum('bhcts,bhcsd->bhctd', scores, vc)
           + inter[..., None] * jnp.einsum('bhcde,bhcte->bhctd', c_prev, qc))
    den = jnp.sum(scores, axis=-1) + inter * jnp.einsum('bhce,bhcte->bhct', n_prev, qc)
    out = num / jnp.maximum(jnp.abs(den), jnp.exp(-m_t))[..., None]
    return out.reshape(b, nh, s, dh)


def mlstm_group(q_pre, k_pre, v_pre, o_pre, gates, conv_w, conv_b, norm_g):
    bsz, s, _ = v_pre.shape
    qk = jax.nn.silu(dwconv3_centred(jnp.concatenate([q_pre, k_pre], axis=-1), conv_w, conv_b))
    q, k = jnp.split(qk, 2, axis=-1)

    def heads(t):
        return t.reshape(bsz, s, MLSTM_HEADS, MLSTM_HEAD_DIM).transpose(0, 2, 1, 3).astype(jnp.float32)

    q = heads(q)
    k = heads(k) * (MLSTM_HEAD_DIM ** -0.5)
    v = heads(v_pre)
    gt = gates.astype(jnp.float32).reshape(bsz, s, 4, MLSTM_HEADS).transpose(2, 0, 3, 1)
    h_fwd = mlstm_chunkwise(q, k, v, gt[0], gt[1])
    flip = lambda t: jnp.flip(t, axis=2)
    h_bwd = flip(mlstm_chunkwise(flip(q), flip(k), flip(v), flip(gt[2]), flip(gt[3])))
    h = (h_fwd + h_bwd).transpose(0, 2, 1, 3)
    h = rms_norm(h, norm_g.reshape(MLSTM_HEADS, MLSTM_HEAD_DIM))
    h = h.reshape(bsz, s, MLSTM_WIDTH).astype(v_pre.dtype)
    return jax.nn.sigmoid(o_pre) * h


def mla_group(c_q, c_kv, k_rope_pre, q_norm_g, w_uq, kv_norm_g, w_ukv, cos, sin):
    bsz, s, _ = c_q.shape
    q = (rms_norm(c_q, q_norm_g) @ w_uq).reshape(bsz, s, MLA_HEADS, QK_NOPE + QK_ROPE)
    q_nope, q_rope = q[..., :QK_NOPE], q[..., QK_NOPE:]
    q_rope = apply_rope(q_rope, cos[:, None, :], sin[:, None, :])
    kv = (rms_norm(c_kv, kv_norm_g) @ w_ukv).reshape(bsz, s, MLA_HEADS, QK_NOPE + V_HEAD)
    k_nope, v = kv[..., :QK_NOPE], kv[..., QK_NOPE:]
    k_rope = apply_rope(k_rope_pre, cos, sin)
    scale = (QK_NOPE + QK_ROPE) ** -0.5
    nb = s // Q_BLOCK

    def to_blocks(t):
        return jnp.moveaxis(t.reshape(bsz, nb, Q_BLOCK, *t.shape[2:]), 1, 0)

    def attend(blk):
        qn, qr = blk
        sc = (jnp.einsum('bqhd,bkhd->bhqk', qn, k_nope)
              + jnp.einsum('bqhr,bkr->bhqk', qr, k_rope))
        pr = jax.nn.softmax(sc.astype(jnp.float32) * scale, axis=-1)
        return jnp.einsum('bhqk,bkhd->bqhd', pr.astype(v.dtype), v)

    o = lax.map(attend, (to_blocks(q_nope), to_blocks(q_rope)))
    return jnp.moveaxis(o, 0, 1).reshape(bsz, s, MLA_WIDTH)


def conv_gated_mlp(xn, w_up, conv_w, conv_b, w_down):
    u = dwconv3_centred(xn @ w_up, conv_w, conv_b)
    gate, val = jnp.split(u, 2, axis=-1)
    return (jax.nn.silu(gate) * val) @ w_down


def setup_inputs(seed: int = 0) -> dict:
    key = jax.random.key(seed)
    ks = jax.random.split(key, 24)
    f32 = jnp.float32

    def nrm(k, shape, scale):
        return jax.random.normal(k, shape, f32) * scale

    def gain(k, shape):
        return 1.0 + 0.02 * jax.random.normal(k, shape, f32)

    gate_i = nrm(ks[4], (DEPTH, 2, MLSTM_HEADS), 0.1)
    gate_f = jnp.linspace(3.0, 6.0, MLSTM_HEADS, dtype=f32) + nrm(ks[5], (DEPTH, 2, MLSTM_HEADS), 0.1)
    b_gates = jnp.stack([gate_i, gate_f], axis=2).reshape(DEPTH, N_GATES)
    return {
        'x': nrm(ks[0], (BATCH, SEQ, D_MODEL), 1.0),
        'p': nrm(ks[1], (DEPTH, BATCH, SEQ, PLE_DIM), 1.0),
        'ln_mix_g': gain(ks[2], (DEPTH, D_MODEL)),
        'w_in': nrm(ks[3], (DEPTH, D_MODEL, IN_COLS), D_MODEL ** -0.5),
        'b_gates': b_gates,
        'conv_qk_w': nrm(ks[6], (DEPTH, CONV_WIDTH, 2 * MLSTM_WIDTH), CONV_WIDTH ** -0.5),
        'conv_qk_b': nrm(ks[7], (DEPTH, 2 * MLSTM_WIDTH), 0.02),
        'mlstm_norm_g': gain(ks[8], (DEPTH, MLSTM_WIDTH)),
        'q_norm_g': gain(ks[9], (DEPTH, Q_LORA)),
        'w_uq': nrm(ks[10], (DEPTH, Q_LORA, MLA_HEADS * (QK_NOPE + QK_ROPE)), Q_LORA ** -0.5),
        'kv_norm_g': gain(ks[11], (DEPTH, KV_LORA)),
        'w_ukv': nrm(ks[12], (DEPTH, KV_LORA, MLA_HEADS * (QK_NOPE + V_HEAD)), KV_LORA ** -0.5),
        'w_out': nrm(ks[13], (DEPTH, D_MIX, D_MODEL), D_MIX ** -0.5),
        'ln_ffn_g': gain(ks[14], (DEPTH, D_MODEL)),
        'w_up': nrm(ks[15], (DEPTH, D_MODEL, 2 * D_FF), D_MODEL ** -0.5),
        'conv_ffn_w': nrm(ks[16], (DEPTH, CONV_WIDTH, 2 * D_FF), CONV_WIDTH ** -0.5),
        'conv_ffn_b': nrm(ks[17], (DEPTH, 2 * D_FF), 0.02),
        'w_down': nrm(ks[18], (DEPTH, D_FF, D_MODEL), D_FF ** -0.5),
        'ple_norm_g': gain(ks[19], (DEPTH, D_MODEL)),
        'w_ple_gate': nrm(ks[20], (DEPTH, D_MODEL, D_MODEL), D_MODEL ** -0.5),
        'w_ple_proj': nrm(ks[21], (DEPTH, PLE_DIM, D_MODEL), PLE_DIM ** -0.5),
        'ple_post_g': gain(ks[22], (DEPTH, D_MODEL)),
        'final_g': gain(ks[23], (D_MODEL,)),
    }


def reference(x, p, ln_mix_g, w_in, b_gates, conv_qk_w, conv_qk_b, mlstm_norm_g,
              q_norm_g, w_uq, kv_norm_g, w_ukv, w_out, ln_ffn_g, w_up, conv_ffn_w,
              conv_ffn_b, w_down, ple_norm_g, w_ple_gate, w_ple_proj, ple_post_g, final_g):
    s = x.shape[1]
    pos = jnp.arange(s, dtype=jnp.float32)
    inv_freq = ROPE_THETA ** (-jnp.arange(0, QK_ROPE, 2, dtype=jnp.float32) / QK_ROPE)
    ang = pos[:, None] * inv_freq[None, :]
    cos, sin = jnp.cos(ang), jnp.sin(ang)
    h = x
    for l in range(DEPTH):
        xn = rms_norm(h, ln_mix_g[l])
        proj = xn @ w_in[l]
        y_a = mlstm_group(proj[..., OFF_Q:OFF_K], proj[..., OFF_K:OFF_V],
                          proj[..., OFF_V:OFF_O], proj[..., OFF_O:OFF_G],
                          proj[..., OFF_G:OFF_CQ] + b_gates[l],
                          conv_qk_w[l], conv_qk_b[l], mlstm_norm_g[l])
        y_b = mla_group(proj[..., OFF_CQ:OFF_CKV], proj[..., OFF_CKV:OFF_KR],
                        proj[..., OFF_KR:IN_COLS], q_norm_g[l], w_uq[l],
                        kv_norm_g[l], w_ukv[l], cos, sin)
        h = h + jnp.concatenate([y_a, y_b], axis=-1) @ w_out[l]
        h = h + conv_gated_mlp(rms_norm(h, ln_ffn_g[l]), w_up[l], conv_ffn_w[l],
                               conv_ffn_b[l], w_down[l])
        gate = jax.nn.sigmoid(rms_norm(h, ple_norm_g[l]) @ w_ple_gate[l])
        h = h + gate * rms_norm(p[l] @ w_ple_proj[l], ple_post_g[l])
    return rms_norm(h, final_g)
```

```python
import functools
import math

import jax
import jax.numpy as jnp
from jax import lax
from jax.experimental import pallas as pl
from jax.experimental.pallas import tpu as pltpu

F32 = jnp.float32
BF16 = jnp.bfloat16

EPS = 1e-6
LANES = 128
HALO_ROWS = 8

MLSTM_HEADS = 4
MLSTM_HEAD_DIM = 128
MLSTM_WIDTH = MLSTM_HEADS * MLSTM_HEAD_DIM
MLSTM_CHUNK = 128
N_GATES = 4 * MLSTM_HEADS
MLA_HEADS = 4
QK_NOPE = 128
QK_ROPE = 64
V_HEAD = 128
Q_LORA = 256
KV_LORA = 128
MLA_WIDTH = MLA_HEADS * V_HEAD
ROPE_THETA = 10000.0
ATT_DIM = 2 * LANES
SOFTMAX_SCALE = (QK_NOPE + QK_ROPE) ** -0.5
LOG2E = math.log2(math.e)

VMEM_LIMIT_BYTES = 56 * 1024 * 1024


def _rms(xf, g):
    ms = jnp.mean(xf * xf, axis=-1, keepdims=True)
    return xf * lax.rsqrt(ms + EPS) * g


def _dot(a, b):
    return jnp.dot(a, b, preferred_element_type=F32)


def _dot_nt(a, b):
    return lax.dot_general(a, b, (((1,), (1,)), ((), ())), preferred_element_type=F32)


def _conv3_rows(u, u_halo, cw, cb, tile, n_tiles):
    tm = u.shape[0]
    prev_row = jnp.where(tile > 0, u_halo[HALO_ROWS - 1:HALO_ROWS, :], 0.0)
    next_row = jnp.where(tile < n_tiles - 1, u_halo[HALO_ROWS:HALO_ROWS + 1, :], 0.0)
    rows = lax.broadcasted_iota(jnp.int32, u.shape, 0)
    up = jnp.where(rows == 0, prev_row, pltpu.roll(u, 1, 0))
    dn = jnp.where(rows == tm - 1, next_row, pltpu.roll(u, tm - 1, 0))
    return up * cw[0:1, :] + u * cw[1:2, :] + dn * cw[2:3, :] + cb


def _in_proj_kernel(x_ref, xp_ref, xn_ref, g_ref, wqk_ref, wv_ref, wo_ref, wg_ref, wgt_ref,
                    bg_ref, bgt_ref, wc_ref, cw_ref, cb_ref, qg_ref, wuqn_ref, wuqa_ref,
                    wuqb_ref, kvg_ref, wukt_ref, cos_ref, sin_ref,
                    qm_ref, km_ref, vm_ref, og_ref, gc_ref, gt_ref, qa_ref, ka_ref,
                    *, n_tiles):
    tile = pl.program_id(1)
    g = g_ref[...]
    xn = _rms(x_ref[0], g).astype(BF16)
    xh = jnp.concatenate([xp_ref[0], xn_ref[0]], axis=0)
    xhn = _rms(xh, g).astype(BF16)

    wqk = wqk_ref[...]
    c = _conv3_rows(_dot(xn, wqk), _dot(xhn, wqk), cw_ref[...], cb_ref[...], tile, n_tiles)
    act = c * jax.nn.sigmoid(c)
    qm_ref[0] = act[:, :MLSTM_WIDTH].astype(BF16)
    km_ref[0] = (act[:, MLSTM_WIDTH:] * (MLSTM_HEAD_DIM ** -0.5)).astype(BF16)
    vm_ref[0] = _dot(xn, wv_ref[...]).astype(BF16)
    og_ref[0] = jax.nn.sigmoid(_dot(xn, wo_ref[...])).astype(BF16)

    gc_ref[0] = _dot(xn, wg_ref[...]) + bg_ref[...]
    gt_ref[0] = _dot_nt(wgt_ref[...], xn) + bgt_ref[...]

    call = _dot(xn, wc_ref[...])
    cq = call[:, :Q_LORA]
    ckv = call[:, Q_LORA:Q_LORA + KV_LORA]
    kra = call[:, Q_LORA + KV_LORA:Q_LORA + KV_LORA + LANES]
    krb = call[:, Q_LORA + KV_LORA + LANES:]
    cos_t = cos_ref[...]
    sin_t = sin_ref[...]
    scale = SOFTMAX_SCALE * LOG2E

    cqn = _rms(cq, qg_ref[...]).astype(BF16)
    qn = _dot(cqn, wuqn_ref[...]).astype(BF16)
    qra = _dot(cqn, wuqa_ref[...])
    qrb = _dot(cqn, wuqb_ref[...])
    for h in range(MLA_HEADS):
        sl = slice(h * LANES, (h + 1) * LANES)
        q_lat = _dot(qn[:, sl], wukt_ref[h])
        q_rope = qra[:, sl] * cos_t + qrb[:, sl] * sin_t
        qa_ref[0, h, :, 0:LANES] = (q_lat * scale).astype(BF16)
        qa_ref[0, h, :, LANES:ATT_DIM] = (q_rope * scale).astype(BF16)

    ka_ref[0, :, 0:LANES] = _rms(ckv, kvg_ref[...]).astype(BF16)
    k_rope = kra * cos_t + krb * sin_t
    lane = lax.broadcasted_iota(jnp.int32, k_rope.shape, 1)
    ka_ref[0, :, LANES:ATT_DIM] = jnp.where(lane >= QK_ROPE, 1.0, k_rope).astype(BF16)


def _in_proj(x, g_mix, w, tm):
    b, s, d = x.shape
    n_tiles = s // tm
    hb = tm // HALO_ROWS
    n_halo = s // HALO_ROWS

    def const(shape):
        return pl.BlockSpec(shape, lambda bi, ti: (0,) * len(shape))

    in_specs = [
        pl.BlockSpec((1, tm, d), lambda bi, ti: (bi, ti, 0)),
        pl.BlockSpec((1, HALO_ROWS, d), lambda bi, ti: (bi, jnp.maximum(ti * hb - 1, 0), 0)),
        pl.BlockSpec((1, HALO_ROWS, d), lambda bi, ti: (bi, jnp.minimum((ti + 1) * hb, n_halo - 1), 0)),
        const((1, d)),
        const(w["wqk"].shape), const(w["wv"].shape), const(w["wo"].shape),
        const(w["wg"].shape), const(w["wgt"].shape), const(w["bg"].shape), const(w["bgt"].shape),
        const(w["wc"].shape), const(w["cw"].shape), const(w["cb"].shape),
        const(w["qg"].shape), const(w["wuqn"].shape), const(w["wuqa"].shape), const(w["wuqb"].shape),
        const(w["kvg"].shape), const(w["wukt"].shape),
        pl.BlockSpec((tm, LANES), lambda bi, ti: (ti, 0)),
        pl.BlockSpec((tm, LANES), lambda bi, ti: (ti, 0)),
    ]
    row_spec = lambda width: pl.BlockSpec((1, tm, width), lambda bi, ti: (bi, ti, 0))
    out_specs = [
        row_spec(MLSTM_WIDTH), row_spec(MLSTM_WIDTH), row_spec(MLSTM_WIDTH), row_spec(MLSTM_WIDTH),
        row_spec(LANES),
        pl.BlockSpec((1, N_GATES, tm), lambda bi, ti: (bi, 0, ti)),
        pl.BlockSpec((1, MLA_HEADS, tm, ATT_DIM), lambda bi, ti: (bi, 0, ti, 0)),
        row_spec(ATT_DIM),
    ]
    out_shape = [
        jax.ShapeDtypeStruct((b, s, MLSTM_WIDTH), BF16),
        jax.ShapeDtypeStruct((b, s, MLSTM_WIDTH), BF16),
        jax.ShapeDtypeStruct((b, s, MLSTM_WIDTH), BF16),
        jax.ShapeDtypeStruct((b, s, MLSTM_WIDTH), BF16),
        jax.ShapeDtypeStruct((b, s, LANES), F32),
        jax.ShapeDtypeStruct((b, N_GATES, s), F32),
        jax.ShapeDtypeStruct((b, MLA_HEADS, s, ATT_DIM), BF16),
        jax.ShapeDtypeStruct((b, s, ATT_DIM), BF16),
    ]
    return pl.pallas_call(
        functools.partial(_in_proj_kernel, n_tiles=n_tiles),
        grid=(b, n_tiles),
        in_specs=in_specs,
        out_specs=out_specs,
        out_shape=out_shape,
        compiler_params=pltpu.CompilerParams(
            dimension_semantics=("parallel", "parallel"), vmem_limit_bytes=VMEM_LIMIT_BYTES),
        name="in_proj",
    )(x, x, x, g_mix, w["wqk"], w["wv"], w["wo"], w["wg"], w["wgt"], w["bg"], w["bgt"], w["wc"],
      w["cw"], w["cb"], w["qg"], w["wuqn"], w["wuqa"], w["wuqb"], w["kvg"], w["wukt"],
      w["cos"], w["sin"])


def _log_sigmoid(x):
    return jnp.minimum(x, 0.0) - jnp.log1p(jnp.exp(-jnp.abs(x)))


def _split_bf16(x):
    hi = x.astype(BF16)
    lo = (x - hi.astype(F32)).astype(BF16)
    return hi, lo


def _mlstm_direction(q_ref, k_ref, v_ref, gc_ref, gt_ref, o_ref, c_ref, m_ref, *, reverse):
    L = MLSTM_CHUNK
    t_idx = lax.broadcasted_iota(jnp.int32, (L, L), 0)
    s_idx = lax.broadcasted_iota(jnp.int32, (L, L), 1)
    lower = s_idx <= t_idx
    upper = s_idx >= t_idx
    sees = upper if reverse else lower
    sees_bf = jnp.where(sees, 1.0, 0.0).astype(BF16)
    sees_t_bf = jnp.where(upper if not reverse else lower, 1.0, 0.0).astype(BF16)
    kind = 2 if reverse else 0

    gcol = gc_ref[0]
    grow = gt_ref[0]
    hi, lo = _split_bf16(_log_sigmoid(gcol))
    cum_col = _dot(sees_bf, hi) + _dot(sees_bf, lo)
    hi, lo = _split_bf16(_log_sigmoid(grow))
    cum_row = _dot(hi, sees_t_bf) + _dot(lo, sees_t_bf)

    ones_aug = jnp.ones((L, MLSTM_HEAD_DIM), BF16)
    last = 0 if reverse else L - 1
    for h in range(MLSTM_HEADS):
        ci = kind * MLSTM_HEADS + h
        cf = (kind + 1) * MLSTM_HEADS + h
        sl = slice(h * MLSTM_HEAD_DIM, (h + 1) * MLSTM_HEAD_DIM)
        g_c = cum_col[:, cf:cf + 1]
        g_r = cum_row[cf:cf + 1, :]
        logi_r = grow[ci:ci + 1, :]
        g_last = g_c[last:last + 1, :]
        m_prev = m_ref[h][:, 0:1]

        qh = q_ref[0, :, sl]
        kh = k_ref[0, :, sl]
        v_aug = jnp.concatenate([v_ref[0, :, sl], ones_aug], axis=1)

        d = jnp.where(sees, g_c - g_r + logi_r, -jnp.inf)
        b_inter = g_c + m_prev
        m_t = jnp.maximum(jnp.max(d, axis=1, keepdims=True), b_inter)
        scores = _dot_nt(qh, kh) * jnp.exp(d - m_t)
        inter = jnp.exp(b_inter - m_t)
        c_aug = c_ref[h]
        tot = _dot(scores.astype(BF16), v_aug) + inter * _dot(qh, c_aug.astype(BF16))
        num = tot[:, :MLSTM_HEAD_DIM]
        den = tot[:, MLSTM_HEAD_DIM:]
        o_ref[0, :, sl] = (num / jnp.maximum(jnp.abs(den), jnp.exp(-m_t))).astype(o_ref.dtype)

        a_r = g_last - g_r + logi_r
        m_loc = jnp.max(a_r, axis=1, keepdims=True)
        w_r = jnp.exp(a_r - m_loc)
        kw = (kh.astype(F32).T * w_r).astype(BF16)
        upd = _dot(kw, v_aug)
        m_new = jnp.maximum(g_last + m_prev, m_loc)
        s_old = jnp.exp(g_last + m_prev - m_new)
        s_new = jnp.exp(m_loc - m_new)
        c_ref[h] = s_old * c_aug + s_new * upd
        m_ref[h] = jnp.broadcast_to(m_new, (1, LANES))


def _mlstm_kernel(qf_ref, kf_ref, vf_ref, gcf_ref, gtf_ref, qb_ref, kb_ref, vb_ref, gcb_ref, gtb_ref,
                  hf_ref, hb_ref, cf_ref, mf_ref, cb_ref, mb_ref):
    @pl.when(pl.program_id(1) == 0)
    def _():
        cf_ref[...] = jnp.zeros_like(cf_ref)
        mf_ref[...] = jnp.zeros_like(mf_ref)
        cb_ref[...] = jnp.zeros_like(cb_ref)
        mb_ref[...] = jnp.zeros_like(mb_ref)

    _mlstm_direction(qf_ref, kf_ref, vf_ref, gcf_ref, gtf_ref, hf_ref, cf_ref, mf_ref, reverse=False)
    _mlstm_direction(qb_ref, kb_ref, vb_ref, gcb_ref, gtb_ref, hb_ref, cb_ref, mb_ref, reverse=True)


def _mlstm(qm, km, vm, gc, gt):
    b, s, _ = qm.shape
    L = MLSTM_CHUNK
    nc = s // L
    fwd = lambda width: pl.BlockSpec((1, L, width), lambda bi, ci: (bi, ci, 0))
    bwd = lambda width: pl.BlockSpec((1, L, width), lambda bi, ci: (bi, nc - 1 - ci, 0))
    in_specs = [
        fwd(MLSTM_WIDTH), fwd(MLSTM_WIDTH), fwd(MLSTM_WIDTH), fwd(LANES),
        pl.BlockSpec((1, N_GATES, L), lambda bi, ci: (bi, 0, ci)),
        bwd(MLSTM_WIDTH), bwd(MLSTM_WIDTH), bwd(MLSTM_WIDTH), bwd(LANES),
        pl.BlockSpec((1, N_GATES, L), lambda bi, ci: (bi, 0, nc - 1 - ci)),
    ]
    state = pltpu.VMEM((MLSTM_HEADS, MLSTM_HEAD_DIM, 2 * MLSTM_HEAD_DIM), F32)
    stab = pltpu.VMEM((MLSTM_HEADS, 1, LANES), F32)
    return pl.pallas_call(
        _mlstm_kernel,
        grid=(b, nc),
        in_specs=in_specs,
        out_specs=[fwd(MLSTM_WIDTH), bwd(MLSTM_WIDTH)],
        out_shape=[jax.ShapeDtypeStruct((b, s, MLSTM_WIDTH), BF16)] * 2,
        scratch_shapes=[state, stab, state, stab],
        compiler_params=pltpu.CompilerParams(
            dimension_semantics=("parallel", "arbitrary"), vmem_limit_bytes=VMEM_LIMIT_BYTES),
        name="mlstm",
    )(qm, km, vm, gc, gt, qm, km, vm, gc, gt)


def _mla_attn_kernel(q_ref, k_ref, wuv_ref, o_ref, m_ref, acc_ref, *, tk):
    heads, tq, _ = q_ref.shape[1:]
    rows = heads * tq
    n_kv = k_ref.shape[1] // tk
    q = q_ref[0].reshape(rows, ATT_DIM)
    m_ref[...] = jnp.full_like(m_ref, -jnp.inf)
    acc_ref[...] = jnp.zeros_like(acc_ref)

    def body(j, carry):
        kv = k_ref[0, pl.ds(pl.multiple_of(j * tk, tk), tk), :]
        s = _dot_nt(q, kv)
        m_prev = m_ref[...]
        m_new = jnp.maximum(m_prev, jnp.max(s, axis=1, keepdims=True))
        alpha = jnp.exp2(m_prev - m_new)
        p = jnp.exp2(s - m_new[:, 0:1]).astype(BF16)
        pv = _dot(p, kv)
        acc_ref[...] = acc_ref[...] * jnp.concatenate([alpha, alpha], axis=1) + pv
        m_ref[...] = m_new
        return carry

    lax.fori_loop(0, n_kv, body, 0)

    acc = acc_ref[...]
    row_sum = acc[:, ATT_DIM - 1:ATT_DIM]
    o_lat = (acc[:, :KV_LORA] / row_sum).astype(BF16)
    for h in range(heads):
        o_ref[0, :, h * V_HEAD:(h + 1) * V_HEAD] = _dot(
            o_lat[h * tq:(h + 1) * tq, :], wuv_ref[h]).astype(o_ref.dtype)


def _mla_attn(qa, ka, wuv, tq, tk):
    b, heads, s, _ = qa.shape
    return pl.pallas_call(
        functools.partial(_mla_attn_kernel, tk=tk),
        grid=(b, s // tq),
        in_specs=[
            pl.BlockSpec((1, heads, tq, ATT_DIM), lambda bi, qi: (bi, 0, qi, 0)),
            pl.BlockSpec((1, s, ATT_DIM), lambda bi, qi: (bi, 0, 0)),
            pl.BlockSpec(wuv.shape, lambda bi, qi: (0, 0, 0)),
        ],
        out_specs=pl.BlockSpec((1, tq, MLA_WIDTH), lambda bi, qi: (bi, qi, 0)),
        out_shape=jax.ShapeDtypeStruct((b, s, MLA_WIDTH), BF16),
        scratch_shapes=[pltpu.VMEM((heads * tq, LANES), F32), pltpu.VMEM((heads * tq, ATT_DIM), F32)],
        compiler_params=pltpu.CompilerParams(
            dimension_semantics=("parallel", "parallel"), vmem_limit_bytes=VMEM_LIMIT_BYTES),
        name="mla_attn",
    )(qa, ka, wuv)


def _out_proj_kernel(hf_ref, hb_ref, og_ref, yb_ref, x_ref, ng_ref, wout_ref, o_ref):
    h = hf_ref[0].astype(F32) + hb_ref[0].astype(F32)
    ng = ng_ref[...]
    parts = []
    for hd in range(MLSTM_HEADS):
        sl = slice(hd * MLSTM_HEAD_DIM, (hd + 1) * MLSTM_HEAD_DIM)
        parts.append(_rms(h[:, sl], ng[:, sl]))
    y_a = (og_ref[0].astype(F32) * jnp.concatenate(parts, axis=1)).astype(BF16)
    y = jnp.concatenate([y_a, yb_ref[0]], axis=1)
    o_ref[0] = x_ref[0] + _dot(y, wout_ref[...])


def _out_proj(hf, hb, og, yb, x, ng, wout, tm):
    b, s, d = x.shape
    row = lambda width: pl.BlockSpec((1, tm, width), lambda bi, ti: (bi, ti, 0))
    return pl.pallas_call(
        _out_proj_kernel,
        grid=(b, s // tm),
        in_specs=[row(MLSTM_WIDTH), row(MLSTM_WIDTH), row(MLSTM_WIDTH), row(MLA_WIDTH), row(d),
                  pl.BlockSpec(ng.shape, lambda bi, ti: (0, 0)),
                  pl.BlockSpec(wout.shape, lambda bi, ti: (0, 0))],
        out_specs=row(d),
        out_shape=jax.ShapeDtypeStruct((b, s, d), F32),
        compiler_params=pltpu.CompilerParams(
            dimension_semantics=("parallel", "parallel"), vmem_limit_bytes=VMEM_LIMIT_BYTES),
        name="out_proj",
    )(hf, hb, og, yb, x, ng, wout)


def _ffn_ple_kernel(h_ref, hp_ref, hn_ref, g_ref, wug_ref, wuv_ref, cwg_ref, cwv_ref, cbg_ref, cbv_ref,
                    wd_ref, p_ref, pg_ref, wpg_ref, wpp_ref, ppg_ref, fg_ref,
                    o_ref, xn_ref, xh_ref, acc_ref, *, n_tiles, final_norm):
    tile = pl.program_id(1)
    f = pl.program_id(2)

    @pl.when(f == 0)
    def _():
        g = g_ref[...]
        xn_ref[...] = _rms(h_ref[0], g).astype(BF16)
        xh = jnp.concatenate([hp_ref[0], hn_ref[0]], axis=0)
        xh_ref[...] = _rms(xh, g).astype(BF16)
        acc_ref[...] = jnp.zeros_like(acc_ref)

    xn = xn_ref[...]
    xh = xh_ref[...]
    wug = wug_ref[...]
    wuv = wuv_ref[...]
    gate = _conv3_rows(_dot(xn, wug), _dot(xh, wug), cwg_ref[...], cbg_ref[...], tile, n_tiles)
    val = _conv3_rows(_dot(xn, wuv), _dot(xh, wuv), cwv_ref[...], cbv_ref[...], tile, n_tiles)
    a = (gate * jax.nn.sigmoid(gate) * val).astype(BF16)
    acc_ref[...] += _dot(a, wd_ref[...])

    @pl.when(f == pl.num_programs(2) - 1)
    def _():
        h2 = h_ref[0] + acc_ref[...]
        gate_p = jax.nn.sigmoid(_dot(_rms(h2, pg_ref[...]).astype(BF16), wpg_ref[...]))
        emb = _rms(_dot(p_ref[0].astype(BF16), wpp_ref[...]), ppg_ref[...])
        h3 = h2 + gate_p * emb
        o_ref[0] = _rms(h3, fg_ref[...]) if final_norm else h3


def _ffn_ple(h1, p, w, tm, tf, final_norm):
    b, s, d = h1.shape
    d_ff = w["wd"].shape[0]
    nf = d_ff // tf
    n_tiles = s // tm
    hb = tm // HALO_ROWS
    n_halo = s // HALO_ROWS
    ple = p.shape[-1]

    def const(shape):
        return pl.BlockSpec(shape, lambda bi, ti, fi: (0,) * len(shape))

    in_specs = [
        pl.BlockSpec((1, tm, d), lambda bi, ti, fi: (bi, ti, 0)),
        pl.BlockSpec((1, HALO_ROWS, d), lambda bi, ti, fi: (bi, jnp.maximum(ti * hb - 1, 0), 0)),
        pl.BlockSpec((1, HALO_ROWS, d), lambda bi, ti, fi: (bi, jnp.minimum((ti + 1) * hb, n_halo - 1), 0)),
        const((1, d)),
        pl.BlockSpec((d, tf), lambda bi, ti, fi: (0, fi)),
        pl.BlockSpec((d, tf), lambda bi, ti, fi: (0, nf + fi)),
        pl.BlockSpec((3, tf), lambda bi, ti, fi: (0, fi)),
        pl.BlockSpec((3, tf), lambda bi, ti, fi: (0, nf + fi)),
        pl.BlockSpec((1, tf), lambda bi, ti, fi: (0, fi)),
        pl.BlockSpec((1, tf), lambda bi, ti, fi: (0, nf + fi)),
        pl.BlockSpec((tf, d), lambda bi, ti, fi: (fi, 0)),
        pl.BlockSpec((1, tm, ple), lambda bi, ti, fi: (bi, ti, 0)),
        const((1, d)), const((d, d)), const((ple, d)), const((1, d)), const((1, d)),
    ]
    return pl.pallas_call(
        functools.partial(_ffn_ple_kernel, n_tiles=n_tiles, final_norm=final_norm),
        grid=(b, n_tiles, nf),
        in_specs=in_specs,
        out_specs=pl.BlockSpec((1, tm, d), lambda bi, ti, fi: (bi, ti, 0)),
        out_shape=jax.ShapeDtypeStruct((b, s, d), F32),
        scratch_shapes=[pltpu.VMEM((tm, d), BF16), pltpu.VMEM((2 * HALO_ROWS, d), BF16),
                        pltpu.VMEM((tm, d), F32)],
        compiler_params=pltpu.CompilerParams(
            dimension_semantics=("parallel", "parallel", "arbitrary"),
            vmem_limit_bytes=VMEM_LIMIT_BYTES),
        name="ffn_ple",
    )(h1, h1, h1, w["g"], w["wu"], w["wu"], w["cw"], w["cw"], w["cb"], w["cb"], w["wd"], p,
      w["pg"], w["wpg"], w["wpp"], w["ppg"], w["fg"])


def _pick_tile(n, target):
    t = min(n, target)
    while n % t:
        t //= 2
    return t


def _pick_ff_tile(d_ff, target):
    units = d_ff // LANES
    best = 1
    for u in range(1, units + 1):
        if units % u == 0 and u * LANES <= target:
            best = u
    return best * LANES


def _rope_tables(s):
    pos = jnp.arange(s, dtype=F32)
    inv_freq = ROPE_THETA ** (-jnp.arange(0, QK_ROPE, 2, dtype=F32) / QK_ROPE)
    ang = pos[:, None] * inv_freq[None, :]
    cos, sin = jnp.cos(ang), jnp.sin(ang)
    pad = jnp.zeros((s, LANES - QK_ROPE), F32)
    return (jnp.concatenate([cos, cos, pad], axis=1), jnp.concatenate([-sin, sin, pad], axis=1))


def _swap_halves(w):
    half = w.shape[-1] // 2
    return jnp.concatenate([w[..., half:], w[..., :half]], axis=-1)


def _pad_lanes(w, width):
    return jnp.concatenate([w, jnp.zeros(w.shape[:-1] + (width - w.shape[-1],), w.dtype)], axis=-1)


def _in_proj_weights(w_in, b_gates, conv_w, conv_b, q_norm_g, w_uq, kv_norm_g, w_ukv, cos_t, sin_t):
    d = w_in.shape[0]
    o_q, o_v, o_o = 0, 2 * MLSTM_WIDTH, 3 * MLSTM_WIDTH
    o_g = 4 * MLSTM_WIDTH
    o_cq = o_g + N_GATES
    o_ckv = o_cq + Q_LORA
    o_kr = o_ckv + KV_LORA
    w_g = w_in[:, o_g:o_cq]
    w_kr = w_in[:, o_kr:o_kr + QK_ROPE]
    wc = jnp.concatenate([w_in[:, o_cq:o_kr], _pad_lanes(w_kr, LANES), _pad_lanes(_swap_halves(w_kr), LANES)],
                         axis=1)
    uq = w_uq.reshape(Q_LORA, MLA_HEADS, QK_NOPE + QK_ROPE)
    uq_rope = uq[:, :, QK_NOPE:]
    ukv = w_ukv.reshape(KV_LORA, MLA_HEADS, QK_NOPE + V_HEAD)
    return {
        "wqk": w_in[:, o_q:o_v].astype(BF16),
        "wv": w_in[:, o_v:o_o].astype(BF16),
        "wo": w_in[:, o_o:o_g].astype(BF16),
        "wg": _pad_lanes(w_g, LANES).astype(BF16),
        "wgt": w_g.T.astype(BF16),
        "bg": _pad_lanes(b_gates[None, :], LANES),
        "bgt": b_gates[:, None],
        "wc": wc.astype(BF16),
        "cw": conv_w, "cb": conv_b[None, :],
        "qg": q_norm_g[None, :],
        "wuqn": uq[:, :, :QK_NOPE].reshape(Q_LORA, MLA_HEADS * QK_NOPE).astype(BF16),
        "wuqa": _pad_lanes(uq_rope, LANES).reshape(Q_LORA, MLA_HEADS * LANES).astype(BF16),
        "wuqb": _pad_lanes(_swap_halves(uq_rope), LANES).reshape(Q_LORA, MLA_HEADS * LANES).astype(BF16),
        "kvg": kv_norm_g[None, :],
        "wukt": jnp.transpose(ukv[:, :, :QK_NOPE], (1, 2, 0)).astype(BF16),
        "cos": cos_t, "sin": sin_t,
    }, jnp.transpose(ukv[:, :, QK_NOPE:], (1, 0, 2)).astype(BF16)


def kernel(x, p, ln_mix_g, w_in, b_gates, conv_qk_w, conv_qk_b, mlstm_norm_g, q_norm_g, w_uq, kv_norm_g, w_ukv, w_out, ln_ffn_g, w_up, conv_ffn_w, conv_ffn_b, w_down, ple_norm_g, w_ple_gate, w_ple_proj, ple_post_g, final_g):
    depth = w_in.shape[0]
    _, s, _ = x.shape
    d_ff = w_down.shape[1]
    tm = _pick_tile(s, 512)
    tq = _pick_tile(s, 256)
    tk = _pick_tile(s, 512)
    tf = _pick_ff_tile(d_ff, 1408)
    cos_t, sin_t = _rope_tables(s)
    h = x
    for l in range(depth):
        w1, wuv = _in_proj_weights(w_in[l], b_gates[l], conv_qk_w[l], conv_qk_b[l], q_norm_g[l], w_uq[l],
                                   kv_norm_g[l], w_ukv[l], cos_t, sin_t)
        qm, km, vm, og, gc, gt, qa, ka = _in_proj(h, ln_mix_g[l][None, :], w1, tm)
        hf, hb = _mlstm(qm, km, vm, gc, gt)
        yb = _mla_attn(qa, ka, wuv, tq, tk)
        h1 = _out_proj(hf, hb, og, yb, h, mlstm_norm_g[l][None, :], w_out[l].astype(BF16), tm)
        w5 = {
            "g": ln_ffn_g[l][None, :], "wu": w_up[l].astype(BF16), "cw": conv_ffn_w[l],
            "cb": conv_ffn_b[l][None, :], "wd": w_down[l].astype(BF16),
            "pg": ple_norm_g[l][None, :], "wpg": w_ple_gate[l].astype(BF16),
            "wpp": w_ple_proj[l].astype(BF16), "ppg": ple_post_g[l][None, :], "fg": final_g[None, :],
        }
        h = _ffn_ple(h1, p[l], w5, tm, tf, final_norm=(l == depth - 1))
    return h
```

```python
import functools
import math

import jax
import jax.numpy as jnp
from jax import lax
from jax.experimental import pallas as pl
from jax.experimental.pallas import tpu as pltpu

F32 = jnp.float32
BF16 = jnp.bfloat16

EPS = 1e-6
LANES = 128
HALO_ROWS = 8

MLSTM_HEADS = 4
MLSTM_HEAD_DIM = 128
MLSTM_WIDTH = MLSTM_HEADS * MLSTM_HEAD_DIM
MLSTM_CHUNK = 128
N_GATES = 4 * MLSTM_HEADS
MLA_HEADS = 4
QK_NOPE = 128
QK_ROPE = 64
V_HEAD = 128
Q_LORA = 256
KV_LORA = 128
MLA_WIDTH = MLA_HEADS * V_HEAD
ROPE_THETA = 10000.0
ATT_DIM = 2 * LANES
SOFTMAX_SCALE = (QK_NOPE + QK_ROPE) ** -0.5
LOG2E = math.log2(math.e)
ATT_CHAIN_ROWS = 256
FF_CHUNK = 256

VMEM_LIMIT_BYTES = 56 * 1024 * 1024


def _rms(xf, g):
    ms = jnp.mean(xf * xf, axis=-1, keepdims=True)
    return xf * lax.rsqrt(ms + EPS) * g


def _dot(a, b):
    return jnp.dot(a, b, preferred_element_type=F32)


def _dot_nt(a, b):
    return lax.dot_general(a, b, (((1,), (1,)), ((), ())), preferred_element_type=F32)


def _silu(x):
    h = 0.5 * x
    return h * jnp.tanh(h) + h


def _rms_rows_with_halo(x, x_prev, x_next, g):
    halo = jnp.concatenate([x_prev, x_next], axis=0)
    return jnp.concatenate([_rms(x, g).astype(BF16), _rms(halo, g).astype(BF16)], axis=0)


def _conv3_rows(u_all, cw, cb, tile, n_tiles):
    tm = u_all.shape[0] - 2 * HALO_ROWS
    u = u_all[:tm]
    prev_row = jnp.where(tile > 0, u_all[tm + HALO_ROWS - 1:tm + HALO_ROWS, :], 0.0)
    next_row = jnp.where(tile < n_tiles - 1, u_all[tm + HALO_ROWS:tm + HALO_ROWS + 1, :], 0.0)
    up = pltpu.roll(u, 1, 0)
    dn = pltpu.roll(u, tm - 1, 0)
    edge = lax.broadcasted_iota(jnp.int32, (HALO_ROWS, u.shape[1]), 0)
    up = jnp.concatenate([jnp.where(edge == 0, prev_row, up[:HALO_ROWS]), up[HALO_ROWS:]], axis=0)
    dn = jnp.concatenate([dn[:tm - HALO_ROWS],
                          jnp.where(edge == HALO_ROWS - 1, next_row, dn[tm - HALO_ROWS:])], axis=0)
    return up * cw[0:1, :] + u * cw[1:2, :] + dn * cw[2:3, :] + cb


def _in_proj_kernel(x_ref, xp_ref, xn_ref, g_ref, wqk_ref, wv_ref, wo_ref, wgt_ref,
                    bgt_ref, wc_ref, cw_ref, cb_ref, qg_ref, wuqn_ref, wuqa_ref,
                    wuqb_ref, kvg_ref, wukt_ref, cos_ref, sin_ref,
                    qm_ref, kt_ref, vm_ref, og_ref, gt_ref, qa_ref, ka_ref,
                    *, n_tiles):
    tile = pl.program_id(1)
    tm = x_ref.shape[1]
    xa = _rms_rows_with_halo(x_ref[0], xp_ref[0], xn_ref[0], g_ref[...])
    xn = xa[:tm]

    act = _silu(_conv3_rows(_dot(xa, wqk_ref[...]), cw_ref[...], cb_ref[...], tile, n_tiles))
    qm_ref[0] = act[:, :MLSTM_WIDTH].astype(BF16)
    kt_ref[0] = (act[:, MLSTM_WIDTH:] * (MLSTM_HEAD_DIM ** -0.5)).T.astype(BF16)
    vm_ref[0] = _dot(xn, wv_ref[...]).astype(BF16)
    og_ref[0] = (0.5 * jnp.tanh(0.5 * _dot(xn, wo_ref[...])) + 0.5).astype(BF16)

    gt_ref[0] = _dot_nt(wgt_ref[...], xn) + bgt_ref[...]

    call = _dot(xn, wc_ref[...])
    cq = call[:, :Q_LORA]
    ckv = call[:, Q_LORA:Q_LORA + KV_LORA]
    kra = call[:, Q_LORA + KV_LORA:Q_LORA + KV_LORA + LANES]
    krb = call[:, Q_LORA + KV_LORA + LANES:]
    cos_t = cos_ref[...]
    sin_t = sin_ref[...]
    scale = SOFTMAX_SCALE * LOG2E

    cqn = _rms(cq, qg_ref[...]).astype(BF16)
    qn = _dot(cqn, wuqn_ref[...]).astype(BF16)
    qra = _dot(cqn, wuqa_ref[...])
    qrb = _dot(cqn, wuqb_ref[...])
    for h in range(MLA_HEADS):
        sl = slice(h * LANES, (h + 1) * LANES)
        q_lat = _dot(qn[:, sl], wukt_ref[h])
        q_rope = qra[:, sl] * cos_t + qrb[:, sl] * sin_t
        qa_ref[0, h, :, 0:LANES] = (q_lat * scale).astype(BF16)
        qa_ref[0, h, :, LANES:ATT_DIM] = (q_rope * scale).astype(BF16)

    ka_ref[0, :, 0:LANES] = _rms(ckv, kvg_ref[...]).astype(BF16)
    k_rope = kra * cos_t + krb * sin_t
    lane = lax.broadcasted_iota(jnp.int32, k_rope.shape, 1)
    ka_ref[0, :, LANES:ATT_DIM] = jnp.where(lane >= QK_ROPE, 1.0, k_rope).astype(BF16)


def _in_proj(x, g_mix, w, tm):
    b, s, d = x.shape
    n_tiles = s // tm
    hb = tm // HALO_ROWS
    n_halo = s // HALO_ROWS

    def const(shape):
        return pl.BlockSpec(shape, lambda bi, ti: (0,) * len(shape))

    in_specs = [
        pl.BlockSpec((1, tm, d), lambda bi, ti: (bi, ti, 0)),
        pl.BlockSpec((1, HALO_ROWS, d), lambda bi, ti: (bi, jnp.maximum(ti * hb - 1, 0), 0)),
        pl.BlockSpec((1, HALO_ROWS, d), lambda bi, ti: (bi, jnp.minimum((ti + 1) * hb, n_halo - 1), 0)),
        const((1, d)),
        const(w["wqk"].shape), const(w["wv"].shape), const(w["wo"].shape),
        const(w["wgt"].shape), const(w["bgt"].shape),
        const(w["wc"].shape), const(w["cw"].shape), const(w["cb"].shape),
        const(w["qg"].shape), const(w["wuqn"].shape), const(w["wuqa"].shape), const(w["wuqb"].shape),
        const(w["kvg"].shape), const(w["wukt"].shape),
        pl.BlockSpec((tm, LANES), lambda bi, ti: (ti, 0)),
        pl.BlockSpec((tm, LANES), lambda bi, ti: (ti, 0)),
    ]
    row_spec = lambda width: pl.BlockSpec((1, tm, width), lambda bi, ti: (bi, ti, 0))
    out_specs = [
        row_spec(MLSTM_WIDTH),
        pl.BlockSpec((1, MLSTM_WIDTH, tm), lambda bi, ti: (bi, 0, ti)),
        row_spec(MLSTM_WIDTH), row_spec(MLSTM_WIDTH),
        pl.BlockSpec((1, N_GATES, tm), lambda bi, ti: (bi, 0, ti)),
        pl.BlockSpec((1, MLA_HEADS, tm, ATT_DIM), lambda bi, ti: (bi, 0, ti, 0)),
        row_spec(ATT_DIM),
    ]
    out_shape = [
        jax.ShapeDtypeStruct((b, s, MLSTM_WIDTH), BF16),
        jax.ShapeDtypeStruct((b, MLSTM_WIDTH, s), BF16),
        jax.ShapeDtypeStruct((b, s, MLSTM_WIDTH), BF16),
        jax.ShapeDtypeStruct((b, s, MLSTM_WIDTH), BF16),
        jax.ShapeDtypeStruct((b, N_GATES, s), F32),
        jax.ShapeDtypeStruct((b, MLA_HEADS, s, ATT_DIM), BF16),
        jax.ShapeDtypeStruct((b, s, ATT_DIM), BF16),
    ]
    return pl.pallas_call(
        functools.partial(_in_proj_kernel, n_tiles=n_tiles),
        grid=(b, n_tiles),
        in_specs=in_specs,
        out_specs=out_specs,
        out_shape=out_shape,
        compiler_params=pltpu.CompilerParams(
            dimension_semantics=("parallel", "parallel"), vmem_limit_bytes=VMEM_LIMIT_BYTES),
        name="in_proj",
    )(x, x, x, g_mix, w["wqk"], w["wv"], w["wo"], w["wgt"], w["bgt"], w["wc"],
      w["cw"], w["cb"], w["qg"], w["wuqn"], w["wuqa"], w["wuqb"], w["kvg"], w["wukt"],
      w["cos"], w["sin"])


def _log_sigmoid(x):
    return jnp.minimum(x, 0.0) - jnp.log1p(jnp.exp(-jnp.abs(x)))


def _split_bf16(x):
    hi = x.astype(BF16)
    lo = (x - hi.astype(F32)).astype(BF16)
    return hi, lo


def _mlstm_direction(q_ref, kt_ref, v_ref, gt_ref, o_ref, c_ref, m_ref, *, reverse):
    L = MLSTM_CHUNK
    t_idx = lax.broadcasted_iota(jnp.int32, (L, L), 0)
    s_idx = lax.broadcasted_iota(jnp.int32, (L, L), 1)
    lower = s_idx <= t_idx
    upper = s_idx >= t_idx
    sees = upper if reverse else lower
    sees_bf = jnp.where(sees, 1.0, 0.0).astype(BF16)
    sees_t_bf = jnp.where(lower if reverse else upper, 1.0, 0.0).astype(BF16)
    sees2_bf = jnp.concatenate([sees_bf, sees_bf], axis=1)
    kind = 2 if reverse else 0

    grow = gt_ref[0]
    hi, lo = _split_bf16(_log_sigmoid(grow))
    cum_row = _dot(hi, sees_t_bf) + _dot(lo, sees_t_bf)
    hi = hi.astype(F32)
    lo = lo.astype(F32)

    ones_aug = jnp.ones((L, MLSTM_HEAD_DIM), BF16)
    last = 0 if reverse else L - 1
    units = []
    for h in range(MLSTM_HEADS):
        ci = kind * MLSTM_HEADS + h
        cf = (kind + 1) * MLSTM_HEADS + h
        sl = slice(h * MLSTM_HEAD_DIM, (h + 1) * MLSTM_HEAD_DIM)
        u = _Unit()
        u.h, u.sl, u.sees = h, sl, sees
        u.o_ref, u.c_ref, u.m_ref = o_ref, c_ref, m_ref
        u.g_r = cum_row[cf:cf + 1, :]
        u.logi_r = grow[ci:ci + 1, :]
        u.g_last = u.g_r[:, last:last + 1]
        u.m_prev = m_ref[h]
        f_rep = jnp.concatenate([jnp.broadcast_to(hi[cf:cf + 1, :], (LANES, L)),
                                 jnp.broadcast_to(lo[cf:cf + 1, :], (LANES, L))], axis=1).astype(BF16)
        u.g_c = _dot_nt(sees2_bf, f_rep)
        u.qh = q_ref[0, :, sl]
        u.kht = kt_ref[0, sl, :]
        u.v_aug = jnp.concatenate([v_ref[0, :, sl], ones_aug], axis=1)
        units.append(u)
    return units


class _Unit:
    pass


def _mlstm_kernel(qf_ref, ktf_ref, vf_ref, gtf_ref, qb_ref, ktb_ref, vb_ref, gtb_ref,
                  hf_ref, hb_ref, cf_ref, mf_ref, cb_ref, mb_ref):
    @pl.when(pl.program_id(1) == 0)
    def _():
        cf_ref[...] = jnp.zeros_like(cf_ref)
        mf_ref[...] = jnp.zeros_like(mf_ref)
        cb_ref[...] = jnp.zeros_like(cb_ref)
        mb_ref[...] = jnp.zeros_like(mb_ref)

    units = (_mlstm_direction(qf_ref, ktf_ref, vf_ref, gtf_ref, hf_ref, cf_ref, mf_ref, reverse=False)
             + _mlstm_direction(qb_ref, ktb_ref, vb_ref, gtb_ref, hb_ref, cb_ref, mb_ref, reverse=True))
    for u in units:
        u.qk = _dot(u.qh, u.kht)
        u.c_aug = u.c_ref[u.h]
        u.qc = _dot(u.qh, u.c_aug.astype(BF16))
    for u in units:
        a_r = u.g_last - u.g_r + u.logi_r
        u.m_loc = jnp.max(a_r, axis=1, keepdims=True)
        w_r = jnp.exp(a_r - u.m_loc)
        kw = (u.kht.astype(F32) * w_r).astype(BF16)
        u.upd = _dot(kw, u.v_aug)
    for u in units:
        d = jnp.where(u.sees, u.g_c - u.g_r + u.logi_r, -jnp.inf)
        b_inter = u.g_c + u.m_prev
        u.m_t = jnp.maximum(jnp.max(d, axis=1, keepdims=True), b_inter)
        u.scores = (u.qk * jnp.exp(d - u.m_t)).astype(BF16)
        u.inter = jnp.exp(b_inter - u.m_t)
    for u in units:
        tot = _dot(u.scores, u.v_aug) + jnp.concatenate([u.inter, u.inter], axis=1) * u.qc
        num = tot[:, :MLSTM_HEAD_DIM]
        den = tot[:, MLSTM_HEAD_DIM:]
        u.o_ref[0, :, u.sl] = (num / jnp.maximum(jnp.abs(den), jnp.exp(-u.m_t))).astype(u.o_ref.dtype)
    for u in units:
        m_new = jnp.maximum(u.g_last + u.m_prev, u.m_loc)
        s_old = jnp.exp(u.g_last + u.m_prev - m_new)
        s_new = jnp.exp(u.m_loc - m_new)
        u.c_ref[u.h] = (jnp.concatenate([s_old, s_old], axis=1) * u.c_aug
                        + jnp.concatenate([s_new, s_new], axis=1) * u.upd)
        u.m_ref[u.h] = m_new


def _mlstm(qm, kt, vm, gt):
    b, s, _ = qm.shape
    L = MLSTM_CHUNK
    nc = s // L
    rows = lambda ci: pl.BlockSpec((1, L, MLSTM_WIDTH), lambda bi, c: (bi, ci(c), 0))
    cols = lambda height, ci: pl.BlockSpec((1, height, L), lambda bi, c: (bi, 0, ci(c)))
    fwd = lambda c: c
    bwd = lambda c: nc - 1 - c
    in_specs = [rows(fwd), cols(MLSTM_WIDTH, fwd), rows(fwd), cols(N_GATES, fwd),
                rows(bwd), cols(MLSTM_WIDTH, bwd), rows(bwd), cols(N_GATES, bwd)]
    state = pltpu.VMEM((MLSTM_HEADS, MLSTM_HEAD_DIM, 2 * MLSTM_HEAD_DIM), F32)
    stab = pltpu.VMEM((MLSTM_HEADS, 1, LANES), F32)
    return pl.pallas_call(
        _mlstm_kernel,
        grid=(b, nc),
        in_specs=in_specs,
        out_specs=[rows(fwd), rows(bwd)],
        out_shape=[jax.ShapeDtypeStruct((b, s, MLSTM_WIDTH), BF16)] * 2,
        scratch_shapes=[state, stab, state, stab],
        compiler_params=pltpu.CompilerParams(
            dimension_semantics=("parallel", "arbitrary"), vmem_limit_bytes=VMEM_LIMIT_BYTES),
        name="mlstm",
    )(qm, kt, vm, gt, qm, kt, vm, gt)


def _mla_attn_kernel(q_ref, k_ref, wuv_ref, o_ref, m_ref, alpha_ref, p_ref, acc_ref, *, tk, rows):
    heads, tq, _ = q_ref.shape[1:]
    n_kv = k_ref.shape[1] // tk
    chains = [(h, r) for h in range(heads) for r in range(tq // rows)]
    m_ref[...] = jnp.full_like(m_ref, -jnp.inf)
    acc_ref[...] = jnp.zeros_like(acc_ref)

    def kv_block(j):
        start = j * tk if isinstance(j, int) else pl.multiple_of(j * tk, tk)
        return k_ref[0, pl.ds(start, tk), :]

    def probs(c, kv, slot):
        h, r = chains[c]
        s = _dot_nt(q_ref[0, h, r * rows:(r + 1) * rows, :], kv)
        m_prev = m_ref[c]
        m_new = jnp.maximum(m_prev, jnp.max(s, axis=1, keepdims=True))
        alpha_ref[slot, c] = jnp.exp2(m_prev - m_new)
        p_ref[slot, c] = jnp.exp2(s - jnp.concatenate([m_new] * (tk // LANES), axis=1)).astype(BF16)
        m_ref[c] = m_new

    def accumulate(c, kv, slot):
        alpha = alpha_ref[slot, c]
        acc_ref[c] = acc_ref[c] * jnp.concatenate([alpha, alpha], axis=1) + _dot(p_ref[slot, c], kv)

    def stage(j_acc, slot_acc, j_probs, slot_probs):
        kv_acc = None if j_acc is None else kv_block(j_acc)
        kv_probs = None if j_probs is None else kv_block(j_probs)
        for c in range(len(chains)):
            if j_acc is not None:
                accumulate(c, kv_acc, slot_acc)
            if j_probs is not None:
                probs(c, kv_probs, slot_probs)

    stage(None, None, 0, 0)
    n_pairs = (n_kv - 1) // 2

    def body(i, carry):
        j = 2 * i + 1
        stage(j - 1, 0, j, 1)
        stage(j, 1, j + 1, 0)
        return carry

    lax.fori_loop(0, n_pairs, body, 0)
    if (n_kv - 1) % 2:
        stage(n_kv - 2, 0, n_kv - 1, 1)
        stage(n_kv - 1, 1, None, None)
    else:
        stage(n_kv - 1, 0, None, None)

    for c, (h, r) in enumerate(chains):
        acc = acc_ref[c]
        o_lat = (acc[:, :KV_LORA] / acc[:, ATT_DIM - 1:ATT_DIM]).astype(BF16)
        o_ref[0, r * rows:(r + 1) * rows, h * V_HEAD:(h + 1) * V_HEAD] = _dot(
            o_lat, wuv_ref[h]).astype(o_ref.dtype)


def _mla_attn(qa, ka, wuv, tq, tk):
    b, heads, s, _ = qa.shape
    rows = min(tq, ATT_CHAIN_ROWS)
    n_chains = heads * (tq // rows)
    return pl.pallas_call(
        functools.partial(_mla_attn_kernel, tk=tk, rows=rows),
        grid=(b, s // tq),
        in_specs=[
            pl.BlockSpec((1, heads, tq, ATT_DIM), lambda bi, qi: (bi, 0, qi, 0)),
            pl.BlockSpec((1, s, ATT_DIM), lambda bi, qi: (bi, 0, 0)),
            pl.BlockSpec(wuv.shape, lambda bi, qi: (0, 0, 0)),
        ],
        out_specs=pl.BlockSpec((1, tq, MLA_WIDTH), lambda bi, qi: (bi, qi, 0)),
        out_shape=jax.ShapeDtypeStruct((b, s, MLA_WIDTH), BF16),
        scratch_shapes=[pltpu.VMEM((n_chains, rows, LANES), F32), pltpu.VMEM((2, n_chains, rows, LANES), F32),
                        pltpu.VMEM((2, n_chains, rows, tk), BF16), pltpu.VMEM((n_chains, rows, ATT_DIM), F32)],
        compiler_params=pltpu.CompilerParams(
            dimension_semantics=("parallel", "parallel"), vmem_limit_bytes=VMEM_LIMIT_BYTES),
        name="mla_attn",
    )(qa, ka, wuv)


def _out_proj_kernel(hf_ref, hb_ref, og_ref, yb_ref, x_ref, ng_ref, wout_ref, o_ref):
    h = hf_ref[0].astype(F32) + hb_ref[0].astype(F32)
    ng = ng_ref[...]
    parts = []
    for hd in range(MLSTM_HEADS):
        sl = slice(hd * MLSTM_HEAD_DIM, (hd + 1) * MLSTM_HEAD_DIM)
        parts.append(_rms(h[:, sl], ng[:, sl]))
    y_a = (og_ref[0].astype(F32) * jnp.concatenate(parts, axis=1)).astype(BF16)
    y = jnp.concatenate([y_a, yb_ref[0]], axis=1)
    o_ref[0] = x_ref[0] + _dot(y, wout_ref[...])


def _out_proj(hf, hb, og, yb, x, ng, wout, tm):
    b, s, d = x.shape
    row = lambda width: pl.BlockSpec((1, tm, width), lambda bi, ti: (bi, ti, 0))
    return pl.pallas_call(
        _out_proj_kernel,
        grid=(b, s // tm),
        in_specs=[row(MLSTM_WIDTH), row(MLSTM_WIDTH), row(MLSTM_WIDTH), row(MLA_WIDTH), row(d),
                  pl.BlockSpec(ng.shape, lambda bi, ti: (0, 0)),
                  pl.BlockSpec(wout.shape, lambda bi, ti: (0, 0))],
        out_specs=row(d),
        out_shape=jax.ShapeDtypeStruct((b, s, d), F32),
        compiler_params=pltpu.CompilerParams(
            dimension_semantics=("parallel", "parallel"), vmem_limit_bytes=VMEM_LIMIT_BYTES),
        name="out_proj",
    )(hf, hb, og, yb, x, ng, wout)


def _ffn_ple_kernel(h_ref, hp_ref, hn_ref, g_ref, wu_ref, cw_ref, cb_ref, wd_ref, p_ref, pg_ref,
                    wpg_ref, wpp_ref, ppg_ref, fg_ref, o_ref, a_ref, *, n_tiles, final_norm):
    tile = pl.program_id(1)
    d_ff = wd_ref.shape[0]
    h1 = h_ref[0]
    xa = _rms_rows_with_halo(h1, hp_ref[0], hn_ref[0], g_ref[...])

    for lo in range(0, d_ff, FF_CHUNK):
        g_sl = slice(lo, lo + FF_CHUNK)
        v_sl = slice(d_ff + lo, d_ff + lo + FF_CHUNK)
        gate = _conv3_rows(_dot(xa, wu_ref[:, g_sl]), cw_ref[:, g_sl], cb_ref[:, g_sl], tile, n_tiles)
        val = _conv3_rows(_dot(xa, wu_ref[:, v_sl]), cw_ref[:, v_sl], cb_ref[:, v_sl], tile, n_tiles)
        a_ref[:, g_sl] = (_silu(gate) * val).astype(BF16)

    h2 = h1 + _dot(a_ref[...], wd_ref[...])
    gate_p = 0.5 * jnp.tanh(0.5 * _dot(_rms(h2, pg_ref[...]).astype(BF16), wpg_ref[...])) + 0.5
    emb = _rms(_dot(p_ref[0].astype(BF16), wpp_ref[...]), ppg_ref[...])
    h3 = h2 + gate_p * emb
    o_ref[0] = _rms(h3, fg_ref[...]) if final_norm else h3


def _ffn_ple(h1, p, w, tm, final_norm):
    b, s, d = h1.shape
    d_ff = w["wd"].shape[0]
    n_tiles = s // tm
    hb = tm // HALO_ROWS
    n_halo = s // HALO_ROWS
    ple = p.shape[-1]

    def const(shape):
        return pl.BlockSpec(shape, lambda bi, ti: (0,) * len(shape), pipeline_mode=pl.Buffered(1))

    in_specs = [
        pl.BlockSpec((1, tm, d), lambda bi, ti: (bi, ti, 0)),
        pl.BlockSpec((1, HALO_ROWS, d), lambda bi, ti: (bi, jnp.maximum(ti * hb - 1, 0), 0)),
        pl.BlockSpec((1, HALO_ROWS, d), lambda bi, ti: (bi, jnp.minimum((ti + 1) * hb, n_halo - 1), 0)),
        const((1, d)), const((d, 2 * d_ff)), const((3, 2 * d_ff)), const((1, 2 * d_ff)), const((d_ff, d)),
        pl.BlockSpec((1, tm, ple), lambda bi, ti: (bi, ti, 0)),
        const((1, d)), const((d, d)), const((ple, d)), const((1, d)), const((1, d)),
    ]
    return pl.pallas_call(
        functools.partial(_ffn_ple_kernel, n_tiles=n_tiles, final_norm=final_norm),
        grid=(b, n_tiles),
        in_specs=in_specs,
        out_specs=pl.BlockSpec((1, tm, d), lambda bi, ti: (bi, ti, 0)),
        out_shape=jax.ShapeDtypeStruct((b, s, d), F32),
        scratch_shapes=[pltpu.VMEM((tm, d_ff), BF16)],
        compiler_params=pltpu.CompilerParams(
            dimension_semantics=("parallel", "parallel"), vmem_limit_bytes=VMEM_LIMIT_BYTES),
        name="ffn_ple",
    )(h1, h1, h1, w["g"], w["wu"], w["cw"], w["cb"], w["wd"], p,
      w["pg"], w["wpg"], w["wpp"], w["ppg"], w["fg"])


def _pick_tile(n, target):
    t = min(n, target)
    while n % t:
        t //= 2
    return t


def _rope_tables(s):
    pos = jnp.arange(s, dtype=F32)
    inv_freq = ROPE_THETA ** (-jnp.arange(0, QK_ROPE, 2, dtype=F32) / QK_ROPE)
    ang = pos[:, None] * inv_freq[None, :]
    cos, sin = jnp.cos(ang), jnp.sin(ang)
    pad = jnp.zeros((s, LANES - QK_ROPE), F32)
    return (jnp.concatenate([cos, cos, pad], axis=1), jnp.concatenate([-sin, sin, pad], axis=1))


def _swap_halves(w):
    half = w.shape[-1] // 2
    return jnp.concatenate([w[..., half:], w[..., :half]], axis=-1)


def _pad_lanes(w, width):
    return jnp.concatenate([w, jnp.zeros(w.shape[:-1] + (width - w.shape[-1],), w.dtype)], axis=-1)


def _in_proj_weights(w_in, b_gates, conv_w, conv_b, q_norm_g, w_uq, kv_norm_g, w_ukv, cos_t, sin_t):
    d = w_in.shape[0]
    o_q, o_v, o_o = 0, 2 * MLSTM_WIDTH, 3 * MLSTM_WIDTH
    o_g = 4 * MLSTM_WIDTH
    o_cq = o_g + N_GATES
    o_ckv = o_cq + Q_LORA
    o_kr = o_ckv + KV_LORA
    w_g = w_in[:, o_g:o_cq]
    w_kr = w_in[:, o_kr:o_kr + QK_ROPE]
    wc = jnp.concatenate([w_in[:, o_cq:o_kr], _pad_lanes(w_kr, LANES), _pad_lanes(_swap_halves(w_kr), LANES)],
                         axis=1)
    uq = w_uq.reshape(Q_LORA, MLA_HEADS, QK_NOPE + QK_ROPE)
    uq_rope = uq[:, :, QK_NOPE:]
    ukv = w_ukv.reshape(KV_LORA, MLA_HEADS, QK_NOPE + V_HEAD)
    return {
        "wqk": w_in[:, o_q:o_v].astype(BF16),
        "wv": w_in[:, o_v:o_o].astype(BF16),
        "wo": w_in[:, o_o:o_g].astype(BF16),
        "wgt": w_g.T.astype(BF16),
        "bgt": b_gates[:, None],
        "wc": wc.astype(BF16),
        "cw": conv_w, "cb": conv_b[None, :],
        "qg": q_norm_g[None, :],
        "wuqn": uq[:, :, :QK_NOPE].reshape(Q_LORA, MLA_HEADS * QK_NOPE).astype(BF16),
        "wuqa": _pad_lanes(uq_rope, LANES).reshape(Q_LORA, MLA_HEADS * LANES).astype(BF16),
        "wuqb": _pad_lanes(_swap_halves(uq_rope), LANES).reshape(Q_LORA, MLA_HEADS * LANES).astype(BF16),
        "kvg": kv_norm_g[None, :],
        "wukt": jnp.transpose(ukv[:, :, :QK_NOPE], (1, 2, 0)).astype(BF16),
        "cos": cos_t, "sin": sin_t,
    }, jnp.transpose(ukv[:, :, QK_NOPE:], (1, 0, 2)).astype(BF16)


def kernel(x, p, ln_mix_g, w_in, b_gates, conv_qk_w, conv_qk_b, mlstm_norm_g, q_norm_g, w_uq, kv_norm_g, w_ukv, w_out, ln_ffn_g, w_up, conv_ffn_w, conv_ffn_b, w_down, ple_norm_g, w_ple_gate, w_ple_proj, ple_post_g, final_g):
    depth = w_in.shape[0]
    _, s, _ = x.shape
    assert w_down.shape[1] % FF_CHUNK == 0
    tm = _pick_tile(s, 512)
    tq = _pick_tile(s, 512)
    tk = _pick_tile(s, 1024)
    cos_t, sin_t = _rope_tables(s)
    h = x
    for l in range(depth):
        w1, wuv = _in_proj_weights(w_in[l], b_gates[l], conv_qk_w[l], conv_qk_b[l], q_norm_g[l], w_uq[l],
                                   kv_norm_g[l], w_ukv[l], cos_t, sin_t)
        qm, kt, vm, og, gt, qa, ka = _in_proj(h, ln_mix_g[l][None, :], w1, tm)
        hf, hb = _mlstm(qm, kt, vm, gt)
        yb = _mla_attn(qa, ka, wuv, tq, tk)
        h1 = _out_proj(hf, hb, og, yb, h, mlstm_norm_g[l][None, :], w_out[l].astype(BF16), tm)
        w5 = {
            "g": ln_ffn_g[l][None, :], "wu": w_up[l].astype(BF16), "cw": conv_ffn_w[l],
            "cb": conv_ffn_b[l][None, :], "wd": w_down[l].astype(BF16),
            "pg": ple_norm_g[l][None, :], "wpg": w_ple_gate[l].astype(BF16),
            "wpp": w_ple_proj[l].astype(BF16), "ppg": ple_post_g[l][None, :], "fg": final_g[None, :],
        }
        h = _ffn_ple(h1, p[l], w5, tm, final_norm=(l == depth - 1))
    return h
```

```python
import functools
import math

import jax
import jax.numpy as jnp
from jax import lax
from jax.experimental import pallas as pl
from jax.experimental.pallas import tpu as pltpu

F32 = jnp.float32
BF16 = jnp.bfloat16

EPS = 1e-6
LANES = 128
HALO_ROWS = 8

MLSTM_HEADS = 4
MLSTM_HEAD_DIM = 128
MLSTM_WIDTH = MLSTM_HEADS * MLSTM_HEAD_DIM
MLSTM_CHUNK = 128
N_GATES = 4 * MLSTM_HEADS
MLA_HEADS = 4
QK_NOPE = 128
QK_ROPE = 64
V_HEAD = 128
Q_LORA = 256
KV_LORA = 128
MLA_WIDTH = MLA_HEADS * V_HEAD
ROPE_THETA = 10000.0
ATT_DIM = 2 * LANES
BF16_SUBLANES = 16
VT_ROWS = KV_LORA + BF16_SUBLANES
SOFTMAX_SCALE = (QK_NOPE + QK_ROPE) ** -0.5
LOG2E = math.log2(math.e)
ATT_CHAIN_ROWS = 256
FF_CHUNK = 256

VMEM_LIMIT_BYTES = 56 * 1024 * 1024


def _rms(xf, g):
    ms = jnp.mean(xf * xf, axis=-1, keepdims=True)
    return xf * lax.rsqrt(ms + EPS) * g


def _dot(a, b):
    return jnp.dot(a, b, preferred_element_type=F32)


def _dot_nt(a, b):
    return lax.dot_general(a, b, (((1,), (1,)), ((), ())), preferred_element_type=F32)


def _silu(x):
    h = 0.5 * x
    return h * jnp.tanh(h) + h


def _rms_rows_with_halo(x, x_prev, x_next, g):
    halo = jnp.concatenate([x_prev, x_next], axis=0)
    return jnp.concatenate([_rms(x, g).astype(BF16), _rms(halo, g).astype(BF16)], axis=0)


def _conv3_rows(u_all, cw, cb, tile, n_tiles):
    tm = u_all.shape[0] - 2 * HALO_ROWS
    u = u_all[:tm]
    prev_row = jnp.where(tile > 0, u_all[tm + HALO_ROWS - 1:tm + HALO_ROWS, :], 0.0)
    next_row = jnp.where(tile < n_tiles - 1, u_all[tm + HALO_ROWS:tm + HALO_ROWS + 1, :], 0.0)
    up = pltpu.roll(u, 1, 0)
    dn = pltpu.roll(u, tm - 1, 0)
    edge = lax.broadcasted_iota(jnp.int32, (HALO_ROWS, u.shape[1]), 0)
    up = jnp.concatenate([jnp.where(edge == 0, prev_row, up[:HALO_ROWS]), up[HALO_ROWS:]], axis=0)
    dn = jnp.concatenate([dn[:tm - HALO_ROWS],
                          jnp.where(edge == HALO_ROWS - 1, next_row, dn[tm - HALO_ROWS:])], axis=0)
    return up * cw[0:1, :] + u * cw[1:2, :] + dn * cw[2:3, :] + cb


def _in_proj_kernel(x_ref, xp_ref, xn_ref, g_ref, wqk_ref, wv_ref, wo_ref, wgt_ref,
                    bgt_ref, wc_ref, cw_ref, cb_ref, qg_ref, wuqn_ref, wuqa_ref,
                    wuqb_ref, kvg_ref, wukt_ref, cos_ref, sin_ref,
                    qm_ref, kt_ref, vm_ref, og_ref, gt_ref, qa_ref, ka_ref, vt_ref,
                    *, n_tiles):
    tile = pl.program_id(1)
    tm = x_ref.shape[1]
    xa = _rms_rows_with_halo(x_ref[0], xp_ref[0], xn_ref[0], g_ref[...])
    xn = xa[:tm]

    act = _silu(_conv3_rows(_dot(xa, wqk_ref[...]), cw_ref[...], cb_ref[...], tile, n_tiles))
    qm_ref[0] = act[:, :MLSTM_WIDTH].astype(BF16)
    kt_ref[0] = (act[:, MLSTM_WIDTH:] * (MLSTM_HEAD_DIM ** -0.5)).T.astype(BF16)
    vm_ref[0] = _dot(xn, wv_ref[...]).astype(BF16)
    og_ref[0] = (0.5 * jnp.tanh(0.5 * _dot(xn, wo_ref[...])) + 0.5).astype(BF16)

    gt_ref[0] = _dot_nt(wgt_ref[...], xn) + bgt_ref[...]

    call = _dot(xn, wc_ref[...])
    cq = call[:, :Q_LORA]
    ckv = call[:, Q_LORA:Q_LORA + KV_LORA]
    kra = call[:, Q_LORA + KV_LORA:Q_LORA + KV_LORA + LANES]
    krb = call[:, Q_LORA + KV_LORA + LANES:]
    cos_t = cos_ref[...]
    sin_t = sin_ref[...]
    scale = SOFTMAX_SCALE * LOG2E

    cqn = _rms(cq, qg_ref[...]).astype(BF16)
    qn = _dot(cqn, wuqn_ref[...]).astype(BF16)
    qra = _dot(cqn, wuqa_ref[...])
    qrb = _dot(cqn, wuqb_ref[...])
    for h in range(MLA_HEADS):
        sl = slice(h * LANES, (h + 1) * LANES)
        q_lat = _dot(qn[:, sl], wukt_ref[h])
        q_rope = qra[:, sl] * cos_t + qrb[:, sl] * sin_t
        qa_ref[0, h, :, 0:LANES] = (q_lat * scale).astype(BF16)
        qa_ref[0, h, :, LANES:ATT_DIM] = (q_rope * scale).astype(BF16)

    ckvn = _rms(ckv, kvg_ref[...])
    ka_ref[0, :, 0:LANES] = ckvn.astype(BF16)
    ka_ref[0, :, LANES:ATT_DIM] = (kra * cos_t + krb * sin_t).astype(BF16)
    vt_ref[0, 0:KV_LORA, :] = ckvn.T.astype(BF16)
    vt_ref[0, KV_LORA:, :] = jnp.ones((VT_ROWS - KV_LORA, tm), BF16)


def _in_proj(x, g_mix, w, tm):
    b, s, d = x.shape
    n_tiles = s // tm
    hb = tm // HALO_ROWS
    n_halo = s // HALO_ROWS

    def const(shape):
        return pl.BlockSpec(shape, lambda bi, ti: (0,) * len(shape))

    in_specs = [
        pl.BlockSpec((1, tm, d), lambda bi, ti: (bi, ti, 0)),
        pl.BlockSpec((1, HALO_ROWS, d), lambda bi, ti: (bi, jnp.maximum(ti * hb - 1, 0), 0)),
        pl.BlockSpec((1, HALO_ROWS, d), lambda bi, ti: (bi, jnp.minimum((ti + 1) * hb, n_halo - 1), 0)),
        const((1, d)),
        const(w["wqk"].shape), const(w["wv"].shape), const(w["wo"].shape),
        const(w["wgt"].shape), const(w["bgt"].shape),
        const(w["wc"].shape), const(w["cw"].shape), const(w["cb"].shape),
        const(w["qg"].shape), const(w["wuqn"].shape), const(w["wuqa"].shape), const(w["wuqb"].shape),
        const(w["kvg"].shape), const(w["wukt"].shape),
        pl.BlockSpec((tm, LANES), lambda bi, ti: (ti, 0)),
        pl.BlockSpec((tm, LANES), lambda bi, ti: (ti, 0)),
    ]
    row_spec = lambda width: pl.BlockSpec((1, tm, width), lambda bi, ti: (bi, ti, 0))
    out_specs = [
        row_spec(MLSTM_WIDTH),
        pl.BlockSpec((1, MLSTM_WIDTH, tm), lambda bi, ti: (bi, 0, ti)),
        row_spec(MLSTM_WIDTH), row_spec(MLSTM_WIDTH),
        pl.BlockSpec((1, N_GATES, tm), lambda bi, ti: (bi, 0, ti)),
        pl.BlockSpec((1, MLA_HEADS, tm, ATT_DIM), lambda bi, ti: (bi, 0, ti, 0)),
        row_spec(ATT_DIM),
        pl.BlockSpec((1, VT_ROWS, tm), lambda bi, ti: (bi, 0, ti)),
    ]
    out_shape = [
        jax.ShapeDtypeStruct((b, s, MLSTM_WIDTH), BF16),
        jax.ShapeDtypeStruct((b, MLSTM_WIDTH, s), BF16),
        jax.ShapeDtypeStruct((b, s, MLSTM_WIDTH), BF16),
        jax.ShapeDtypeStruct((b, s, MLSTM_WIDTH), BF16),
        jax.ShapeDtypeStruct((b, N_GATES, s), F32),
        jax.ShapeDtypeStruct((b, MLA_HEADS, s, ATT_DIM), BF16),
        jax.ShapeDtypeStruct((b, s, ATT_DIM), BF16),
        jax.ShapeDtypeStruct((b, VT_ROWS, s), BF16),
    ]
    return pl.pallas_call(
        functools.partial(_in_proj_kernel, n_tiles=n_tiles),
        grid=(b, n_tiles),
        in_specs=in_specs,
        out_specs=out_specs,
        out_shape=out_shape,
        compiler_params=pltpu.CompilerParams(
            dimension_semantics=("parallel", "parallel"), vmem_limit_bytes=VMEM_LIMIT_BYTES),
        name="in_proj",
    )(x, x, x, g_mix, w["wqk"], w["wv"], w["wo"], w["wgt"], w["bgt"], w["wc"],
      w["cw"], w["cb"], w["qg"], w["wuqn"], w["wuqa"], w["wuqb"], w["kvg"], w["wukt"],
      w["cos"], w["sin"])


def _log_sigmoid(x):
    return jnp.minimum(x, 0.0) - jnp.log1p(jnp.exp(-jnp.abs(x)))


def _split_bf16(x):
    hi = x.astype(BF16)
    lo = (x - hi.astype(F32)).astype(BF16)
    return hi, lo


def _mlstm_direction(q_ref, kt_ref, v_ref, gt_ref, o_ref, c_ref, m_ref, *, reverse):
    L = MLSTM_CHUNK
    t_idx = lax.broadcasted_iota(jnp.int32, (L, L), 0)
    s_idx = lax.broadcasted_iota(jnp.int32, (L, L), 1)
    lower = s_idx <= t_idx
    upper = s_idx >= t_idx
    sees = upper if reverse else lower
    sees_bf = jnp.where(sees, 1.0, 0.0).astype(BF16)
    sees_t_bf = jnp.where(lower if reverse else upper, 1.0, 0.0).astype(BF16)
    sees2_bf = jnp.concatenate([sees_bf, sees_bf], axis=1)
    kind = 2 if reverse else 0

    grow = gt_ref[0]
    hi, lo = _split_bf16(_log_sigmoid(grow))
    cum_row = _dot(hi, sees_t_bf) + _dot(lo, sees_t_bf)
    hi = hi.astype(F32)
    lo = lo.astype(F32)

    ones_aug = jnp.ones((L, MLSTM_HEAD_DIM), BF16)
    last = 0 if reverse else L - 1
    units = []
    for h in range(MLSTM_HEADS):
        ci = kind * MLSTM_HEADS + h
        cf = (kind + 1) * MLSTM_HEADS + h
        sl = slice(h * MLSTM_HEAD_DIM, (h + 1) * MLSTM_HEAD_DIM)
        u = _Unit()
        u.h, u.sl, u.sees = h, sl, sees
        u.o_ref, u.c_ref, u.m_ref = o_ref, c_ref, m_ref
        u.g_r = cum_row[cf:cf + 1, :]
        u.logi_r = grow[ci:ci + 1, :]
        u.g_last = u.g_r[:, last:last + 1]
        u.m_prev = m_ref[h]
        f_rep = jnp.concatenate([jnp.broadcast_to(hi[cf:cf + 1, :], (LANES, L)),
                                 jnp.broadcast_to(lo[cf:cf + 1, :], (LANES, L))], axis=1).astype(BF16)
        u.g_c = _dot_nt(sees2_bf, f_rep)
        u.qh = q_ref[0, :, sl]
        u.kht = kt_ref[0, sl, :]
        u.v_aug = jnp.concatenate([v_ref[0, :, sl], ones_aug], axis=1)
        units.append(u)
    return units


class _Unit:
    pass


def _mlstm_kernel(qf_ref, ktf_ref, vf_ref, gtf_ref, qb_ref, ktb_ref, vb_ref, gtb_ref,
                  hf_ref, hb_ref, cf_ref, mf_ref, cb_ref, mb_ref):
    @pl.when(pl.program_id(1) == 0)
    def _():
        cf_ref[...] = jnp.zeros_like(cf_ref)
        mf_ref[...] = jnp.zeros_like(mf_ref)
        cb_ref[...] = jnp.zeros_like(cb_ref)
        mb_ref[...] = jnp.zeros_like(mb_ref)

    units = (_mlstm_direction(qf_ref, ktf_ref, vf_ref, gtf_ref, hf_ref, cf_ref, mf_ref, reverse=False)
             + _mlstm_direction(qb_ref, ktb_ref, vb_ref, gtb_ref, hb_ref, cb_ref, mb_ref, reverse=True))
    for u in units:
        u.qk = _dot(u.qh, u.kht)
        u.c_aug = u.c_ref[u.h]
        u.qc = _dot(u.qh, u.c_aug.astype(BF16))
    for u in units:
        a_r = u.g_last - u.g_r + u.logi_r
        u.m_loc = jnp.max(a_r, axis=1, keepdims=True)
        w_r = jnp.exp(a_r - u.m_loc)
        kw = (u.kht.astype(F32) * w_r).astype(BF16)
        u.upd = _dot(kw, u.v_aug)
    for u in units:
        d = jnp.where(u.sees, u.g_c - u.g_r + u.logi_r, -jnp.inf)
        b_inter = u.g_c + u.m_prev
        u.m_t = jnp.maximum(jnp.max(d, axis=1, keepdims=True), b_inter)
        u.scores = (u.qk * jnp.exp(d - u.m_t)).astype(BF16)
        u.inter = jnp.exp(b_inter - u.m_t)
    for u in units:
        tot = _dot(u.scores, u.v_aug) + jnp.concatenate([u.inter, u.inter], axis=1) * u.qc
        num = tot[:, :MLSTM_HEAD_DIM]
        den = tot[:, MLSTM_HEAD_DIM:]
        u.o_ref[0, :, u.sl] = (num / jnp.maximum(jnp.abs(den), jnp.exp(-u.m_t))).astype(u.o_ref.dtype)
    for u in units:
        m_new = jnp.maximum(u.g_last + u.m_prev, u.m_loc)
        s_old = jnp.exp(u.g_last + u.m_prev - m_new)
        s_new = jnp.exp(u.m_loc - m_new)
        u.c_ref[u.h] = (jnp.concatenate([s_old, s_old], axis=1) * u.c_aug
                        + jnp.concatenate([s_new, s_new], axis=1) * u.upd)
        u.m_ref[u.h] = m_new


def _mlstm(qm, kt, vm, gt):
    b, s, _ = qm.shape
    L = MLSTM_CHUNK
    nc = s // L
    rows = lambda ci: pl.BlockSpec((1, L, MLSTM_WIDTH), lambda bi, c: (bi, ci(c), 0))
    cols = lambda height, ci: pl.BlockSpec((1, height, L), lambda bi, c: (bi, 0, ci(c)))
    fwd = lambda c: c
    bwd = lambda c: nc - 1 - c
    in_specs = [rows(fwd), cols(MLSTM_WIDTH, fwd), rows(fwd), cols(N_GATES, fwd),
                rows(bwd), cols(MLSTM_WIDTH, bwd), rows(bwd), cols(N_GATES, bwd)]
    state = pltpu.VMEM((MLSTM_HEADS, MLSTM_HEAD_DIM, 2 * MLSTM_HEAD_DIM), F32)
    stab = pltpu.VMEM((MLSTM_HEADS, 1, LANES), F32)
    return pl.pallas_call(
        _mlstm_kernel,
        grid=(b, nc),
        in_specs=in_specs,
        out_specs=[rows(fwd), rows(bwd)],
        out_shape=[jax.ShapeDtypeStruct((b, s, MLSTM_WIDTH), BF16)] * 2,
        scratch_shapes=[state, stab, state, stab],
        compiler_params=pltpu.CompilerParams(
            dimension_semantics=("parallel", "arbitrary"), vmem_limit_bytes=VMEM_LIMIT_BYTES),
        name="mlstm",
    )(qm, kt, vm, gt, qm, kt, vm, gt)


def _mla_attn_kernel(q_ref, k_ref, vt_ref, wuvt_ref, hf_ref, hb_ref, og_ref, x_ref, ng_ref, wout_ref,
                     o_ref, m_ref, alpha_ref, s_ref, p_ref, acc_ref, y_ref, *, tk, rows):
    heads, tq, _ = q_ref.shape[1:]
    n_kv = k_ref.shape[1] // tk
    chains = [(h, r) for h in range(heads) for r in range(tq // rows)]
    m_ref[...] = jnp.full_like(m_ref, -jnp.inf)
    acc_ref[...] = jnp.zeros_like(acc_ref)

    def block_start(j):
        return j * tk if isinstance(j, int) else pl.multiple_of(j * tk, tk)

    def scores(c, kv, slot):
        h, r = chains[c]
        s_ref[slot, c] = _dot_nt(kv, q_ref[0, h, r * rows:(r + 1) * rows, :])

    def probs(c, slot):
        s = s_ref[slot, c]
        m_prev = m_ref[c]
        m_new = jnp.maximum(m_prev, jnp.max(s, axis=0, keepdims=True))
        alpha_ref[slot, c] = jnp.exp2(m_prev - m_new)
        p_ref[slot, c] = jnp.exp2(s - m_new).astype(BF16)
        m_ref[c] = m_new

    def accumulate(c, vt, slot):
        acc_ref[c] = acc_ref[c] * alpha_ref[slot, c] + _dot(vt, p_ref[slot, c])

    def stage(t, parity):
        static = isinstance(t, int)
        do_scores = (not static) or t < n_kv
        do_probs = (not static) or 1 <= t <= n_kv
        do_acc = (not static) or 2 <= t
        kv = k_ref[0, pl.ds(block_start(t), tk), :] if do_scores else None
        vt = vt_ref[0, :, pl.ds(block_start(t - 2), tk)] if do_acc else None
        for c in range(len(chains)):
            if do_acc:
                accumulate(c, vt, parity)
            if do_scores:
                scores(c, kv, parity)
            if do_probs:
                probs(c, 1 - parity)

    for t in range(min(2, n_kv + 2)):
        stage(t, t % 2)
    n_pairs = max(n_kv - 2, 0) // 2

    def body(i, carry):
        t = 2 * i + 2
        stage(t, 0)
        stage(t + 1, 1)
        return carry

    lax.fori_loop(0, n_pairs, body, 0)
    for t in range(2 + 2 * n_pairs, n_kv + 2):
        stage(t, t % 2)

    h_ml = hf_ref[0].astype(F32) + hb_ref[0].astype(F32)
    ng = ng_ref[...]
    for hd in range(MLSTM_HEADS):
        sl = slice(hd * MLSTM_HEAD_DIM, (hd + 1) * MLSTM_HEAD_DIM)
        y_ref[:, sl] = (og_ref[0, :, sl].astype(F32) * _rms(h_ml[:, sl], ng[:, sl])).astype(BF16)
    for c, (h, r) in enumerate(chains):
        acc = acc_ref[c]
        o_lat_t = (acc[:KV_LORA, :] / acc[KV_LORA:KV_LORA + 1, :]).astype(BF16)
        y_t = _dot(wuvt_ref[h], o_lat_t)
        lo = MLSTM_WIDTH + h * V_HEAD
        y_ref[r * rows:(r + 1) * rows, lo:lo + V_HEAD] = y_t.T.astype(BF16)
    o_ref[0] = x_ref[0] + _dot(y_ref[...], wout_ref[...])


def _mla_attn_out(qa, ka, vt, wuvt, hf, hb, og, x, ng, wout, tq, tk):
    b, heads, s, _ = qa.shape
    d = x.shape[-1]
    rows = min(tq, ATT_CHAIN_ROWS)
    n_chains = heads * (tq // rows)
    row = lambda width: pl.BlockSpec((1, tq, width), lambda bi, qi: (bi, qi, 0))

    def resident(shape, index_map):
        return pl.BlockSpec(shape, index_map, pipeline_mode=pl.Buffered(1))

    return pl.pallas_call(
        functools.partial(_mla_attn_kernel, tk=tk, rows=rows),
        grid=(b, s // tq),
        in_specs=[
            pl.BlockSpec((1, heads, tq, ATT_DIM), lambda bi, qi: (bi, 0, qi, 0)),
            resident((1, s, ATT_DIM), lambda bi, qi: (bi, 0, 0)),
            resident((1, VT_ROWS, s), lambda bi, qi: (bi, 0, 0)),
            resident(wuvt.shape, lambda bi, qi: (0, 0, 0)),
            row(MLSTM_WIDTH), row(MLSTM_WIDTH), row(MLSTM_WIDTH), row(d),
            resident(ng.shape, lambda bi, qi: (0, 0)),
            resident(wout.shape, lambda bi, qi: (0, 0)),
        ],
        out_specs=row(d),
        out_shape=jax.ShapeDtypeStruct((b, s, d), F32),
        scratch_shapes=[pltpu.VMEM((n_chains, 1, rows), F32), pltpu.VMEM((2, n_chains, 1, rows), F32),
                        pltpu.VMEM((2, n_chains, tk, rows), F32), pltpu.VMEM((2, n_chains, tk, rows), BF16),
                        pltpu.VMEM((n_chains, VT_ROWS, rows), F32),
                        pltpu.VMEM((tq, MLSTM_WIDTH + MLA_WIDTH), BF16)],
        compiler_params=pltpu.CompilerParams(
            dimension_semantics=("parallel", "parallel"), vmem_limit_bytes=VMEM_LIMIT_BYTES),
        name="mla_attn_out",
    )(qa, ka, vt, wuvt, hf, hb, og, x, ng, wout)


def _ffn_ple_kernel(h_ref, hp_ref, hn_ref, g_ref, wu_ref, cw_ref, cb_ref, wd_ref, p_ref, pg_ref,
                    wpg_ref, wpp_ref, ppg_ref, fg_ref, o_ref, a_ref, *, n_tiles, final_norm):
    tile = pl.program_id(1)
    d_ff = wd_ref.shape[0]
    h1 = h_ref[0]
    xa = _rms_rows_with_halo(h1, hp_ref[0], hn_ref[0], g_ref[...])

    for lo in range(0, d_ff, FF_CHUNK):
        g_sl = slice(lo, lo + FF_CHUNK)
        v_sl = slice(d_ff + lo, d_ff + lo + FF_CHUNK)
        gate = _conv3_rows(_dot(xa, wu_ref[:, g_sl]), cw_ref[:, g_sl], cb_ref[:, g_sl], tile, n_tiles)
        val = _conv3_rows(_dot(xa, wu_ref[:, v_sl]), cw_ref[:, v_sl], cb_ref[:, v_sl], tile, n_tiles)
        a_ref[:, g_sl] = (_silu(gate) * val).astype(BF16)

    h2 = h1 + _dot(a_ref[...], wd_ref[...])
    gate_p = 0.5 * jnp.tanh(0.5 * _dot(_rms(h2, pg_ref[...]).astype(BF16), wpg_ref[...])) + 0.5
    emb = _rms(_dot(p_ref[0].astype(BF16), wpp_ref[...]), ppg_ref[...])
    h3 = h2 + gate_p * emb
    o_ref[0] = _rms(h3, fg_ref[...]) if final_norm else h3


def _ffn_ple(h1, p, w, tm, final_norm):
    b, s, d = h1.shape
    d_ff = w["wd"].shape[0]
    n_tiles = s // tm
    hb = tm // HALO_ROWS
    n_halo = s // HALO_ROWS
    ple = p.shape[-1]

    def const(shape):
        return pl.BlockSpec(shape, lambda bi, ti: (0,) * len(shape), pipeline_mode=pl.Buffered(1))

    in_specs = [
        pl.BlockSpec((1, tm, d), lambda bi, ti: (bi, ti, 0)),
        pl.BlockSpec((1, HALO_ROWS, d), lambda bi, ti: (bi, jnp.maximum(ti * hb - 1, 0), 0)),
        pl.BlockSpec((1, HALO_ROWS, d), lambda bi, ti: (bi, jnp.minimum((ti + 1) * hb, n_halo - 1), 0)),
        const((1, d)), const((d, 2 * d_ff)), const((3, 2 * d_ff)), const((1, 2 * d_ff)), const((d_ff, d)),
        pl.BlockSpec((1, tm, ple), lambda bi, ti: (bi, ti, 0)),
        const((1, d)), const((d, d)), const((ple, d)), const((1, d)), const((1, d)),
    ]
    return pl.pallas_call(
        functools.partial(_ffn_ple_kernel, n_tiles=n_tiles, final_norm=final_norm),
        grid=(b, n_tiles),
        in_specs=in_specs,
        out_specs=pl.BlockSpec((1, tm, d), lambda bi, ti: (bi, ti, 0)),
        out_shape=jax.ShapeDtypeStruct((b, s, d), F32),
        scratch_shapes=[pltpu.VMEM((tm, d_ff), BF16)],
        compiler_params=pltpu.CompilerParams(
            dimension_semantics=("parallel", "parallel"), vmem_limit_bytes=VMEM_LIMIT_BYTES),
        name="ffn_ple",
    )(h1, h1, h1, w["g"], w["wu"], w["cw"], w["cb"], w["wd"], p,
      w["pg"], w["wpg"], w["wpp"], w["ppg"], w["fg"])


def _pick_tile(n, target):
    t = min(n, target)
    while n % t:
        t //= 2
    return t


def _rope_tables(s):
    pos = jnp.arange(s, dtype=F32)
    inv_freq = ROPE_THETA ** (-jnp.arange(0, QK_ROPE, 2, dtype=F32) / QK_ROPE)
    ang = pos[:, None] * inv_freq[None, :]
    cos, sin = jnp.cos(ang), jnp.sin(ang)
    pad = jnp.zeros((s, LANES - QK_ROPE), F32)
    return (jnp.concatenate([cos, cos, pad], axis=1), jnp.concatenate([-sin, sin, pad], axis=1))


def _swap_halves(w):
    half = w.shape[-1] // 2
    return jnp.concatenate([w[..., half:], w[..., :half]], axis=-1)


def _pad_lanes(w, width):
    return jnp.concatenate([w, jnp.zeros(w.shape[:-1] + (width - w.shape[-1],), w.dtype)], axis=-1)


def _in_proj_weights(w_in, b_gates, conv_w, conv_b, q_norm_g, w_uq, kv_norm_g, w_ukv, cos_t, sin_t):
    d = w_in.shape[0]
    o_q, o_v, o_o = 0, 2 * MLSTM_WIDTH, 3 * MLSTM_WIDTH
    o_g = 4 * MLSTM_WIDTH
    o_cq = o_g + N_GATES
    o_ckv = o_cq + Q_LORA
    o_kr = o_ckv + KV_LORA
    w_g = w_in[:, o_g:o_cq]
    w_kr = w_in[:, o_kr:o_kr + QK_ROPE]
    wc = jnp.concatenate([w_in[:, o_cq:o_kr], _pad_lanes(w_kr, LANES), _pad_lanes(_swap_halves(w_kr), LANES)],
                         axis=1)
    uq = w_uq.reshape(Q_LORA, MLA_HEADS, QK_NOPE + QK_ROPE)
    uq_rope = uq[:, :, QK_NOPE:]
    ukv = w_ukv.reshape(KV_LORA, MLA_HEADS, QK_NOPE + V_HEAD)
    return {
        "wqk": w_in[:, o_q:o_v].astype(BF16),
        "wv": w_in[:, o_v:o_o].astype(BF16),
        "wo": w_in[:, o_o:o_g].astype(BF16),
        "wgt": w_g.T.astype(BF16),
        "bgt": b_gates[:, None],
        "wc": wc.astype(BF16),
        "cw": conv_w, "cb": conv_b[None, :],
        "qg": q_norm_g[None, :],
        "wuqn": uq[:, :, :QK_NOPE].reshape(Q_LORA, MLA_HEADS * QK_NOPE).astype(BF16),
        "wuqa": _pad_lanes(uq_rope, LANES).reshape(Q_LORA, MLA_HEADS * LANES).astype(BF16),
        "wuqb": _pad_lanes(_swap_halves(uq_rope), LANES).reshape(Q_LORA, MLA_HEADS * LANES).astype(BF16),
        "kvg": kv_norm_g[None, :],
        "wukt": jnp.transpose(ukv[:, :, :QK_NOPE], (1, 2, 0)).astype(BF16),
        "cos": cos_t, "sin": sin_t,
    }, jnp.transpose(ukv[:, :, QK_NOPE:], (1, 2, 0)).astype(BF16)


def kernel(x, p, ln_mix_g, w_in, b_gates, conv_qk_w, conv_qk_b, mlstm_norm_g, q_norm_g, w_uq, kv_norm_g, w_ukv, w_out, ln_ffn_g, w_up, conv_ffn_w, conv_ffn_b, w_down, ple_norm_g, w_ple_gate, w_ple_proj, ple_post_g, final_g):
    depth = w_in.shape[0]
    _, s, _ = x.shape
    assert w_down.shape[1] % FF_CHUNK == 0
    tm = _pick_tile(s, 512)
    tq = _pick_tile(s, 512)
    tk = _pick_tile(s, 1024)
    cos_t, sin_t = _rope_tables(s)
    h = x
    for l in range(depth):
        w1, wuvt = _in_proj_weights(w_in[l], b_gates[l], conv_qk_w[l], conv_qk_b[l], q_norm_g[l], w_uq[l],
                                    kv_norm_g[l], w_ukv[l], cos_t, sin_t)
        qm, kt, vm, og, gt, qa, ka, vt = _in_proj(h, ln_mix_g[l][None, :], w1, _pick_tile(s, 1024))
        hf, hb = _mlstm(qm, kt, vm, gt)
        h1 = _mla_attn_out(qa, ka, vt, wuvt, hf, hb, og, h, mlstm_norm_g[l][None, :],
                           w_out[l].astype(BF16), tq, tk)
        w5 = {
            "g": ln_ffn_g[l][None, :], "wu": w_up[l].astype(BF16), "cw": conv_ffn_w[l],
            "cb": conv_ffn_b[l][None, :], "wd": w_down[l].astype(BF16),
            "pg": ple_norm_g[l][None, :], "wpg": w_ple_gate[l].astype(BF16),
            "wpp": w_ple_proj[l].astype(BF16), "ppg": ple_post_g[l][None, :], "fg": final_g[None, :],
        }
        h = _ffn_ple(h1, p[l], w5, tm, final_norm=(l == depth - 1))
    return h
```

```python
import functools
import math

import jax
import jax.numpy as jnp
from jax import lax
from jax.experimental import pallas as pl
from jax.experimental.pallas import tpu as pltpu

F32 = jnp.float32
BF16 = jnp.bfloat16

EPS = 1e-6
LANES = 128
HALO_ROWS = 8

MLSTM_HEADS = 4
MLSTM_HEAD_DIM = 128
MLSTM_WIDTH = MLSTM_HEADS * MLSTM_HEAD_DIM
MLSTM_CHUNK = 128
N_GATES = 4 * MLSTM_HEADS
MLA_HEADS = 4
QK_NOPE = 128
QK_ROPE = 64
V_HEAD = 128
Q_LORA = 256
KV_LORA = 128
MLA_WIDTH = MLA_HEADS * V_HEAD
ROPE_THETA = 10000.0
ATT_DIM = 2 * LANES
SOFTMAX_SCALE = (QK_NOPE + QK_ROPE) ** -0.5
LOG2E = math.log2(math.e)
ATT_CHAIN_ROWS = 256
FF_CHUNK = 256

VMEM_LIMIT_BYTES = 56 * 1024 * 1024


def _rms(xf, g):
    ms = jnp.mean(xf * xf, axis=-1, keepdims=True)
    return xf * lax.rsqrt(ms + EPS) * g


def _dot(a, b):
    return jnp.dot(a, b, preferred_element_type=F32)


def _dot_nt(a, b):
    return lax.dot_general(a, b, (((1,), (1,)), ((), ())), preferred_element_type=F32)


def _silu(x):
    h = 0.5 * x
    return h * jnp.tanh(h) + h


def _rms_rows_with_halo(x, x_prev, x_next, g):
    halo = jnp.concatenate([x_prev, x_next], axis=0)
    return jnp.concatenate([_rms(x, g).astype(BF16), _rms(halo, g).astype(BF16)], axis=0)


def _conv3_rows(u_all, cw, cb, tile, n_tiles):
    tm = u_all.shape[0] - 2 * HALO_ROWS
    u = u_all[:tm]
    prev_row = jnp.where(tile > 0, u_all[tm + HALO_ROWS - 1:tm + HALO_ROWS, :], 0.0)
    next_row = jnp.where(tile < n_tiles - 1, u_all[tm + HALO_ROWS:tm + HALO_ROWS + 1, :], 0.0)
    up = pltpu.roll(u, 1, 0)
    dn = pltpu.roll(u, tm - 1, 0)
    edge = lax.broadcasted_iota(jnp.int32, (HALO_ROWS, u.shape[1]), 0)
    up = jnp.concatenate([jnp.where(edge == 0, prev_row, up[:HALO_ROWS]), up[HALO_ROWS:]], axis=0)
    dn = jnp.concatenate([dn[:tm - HALO_ROWS],
                          jnp.where(edge == HALO_ROWS - 1, next_row, dn[tm - HALO_ROWS:])], axis=0)
    return up * cw[0:1, :] + u * cw[1:2, :] + dn * cw[2:3, :] + cb


def _in_proj_kernel(x_ref, xp_ref, xn_ref, g_ref, wqk_ref, wv_ref, wo_ref, wgt_ref,
                    bgt_ref, wc_ref, cw_ref, cb_ref, qg_ref, wuqn_ref, wuqa_ref,
                    wuqb_ref, kvg_ref, wukt_ref, cos_ref, sin_ref,
                    qm_ref, kt_ref, vm_ref, og_ref, gt_ref, qa_ref, ka_ref,
                    *, n_tiles):
    tile = pl.program_id(1)
    tm = x_ref.shape[1]
    xa = _rms_rows_with_halo(x_ref[0], xp_ref[0], xn_ref[0], g_ref[...])
    xn = xa[:tm]

    act = _silu(_conv3_rows(_dot(xa, wqk_ref[...]), cw_ref[...], cb_ref[...], tile, n_tiles))
    qm_ref[0] = act[:, :MLSTM_WIDTH].astype(BF16)
    kt_ref[0] = (act[:, MLSTM_WIDTH:] * (MLSTM_HEAD_DIM ** -0.5)).T.astype(BF16)
    vm_ref[0] = _dot(xn, wv_ref[...]).astype(BF16)
    og_ref[0] = (0.5 * jnp.tanh(0.5 * _dot(xn, wo_ref[...])) + 0.5).astype(BF16)

    gt_ref[0] = _dot_nt(wgt_ref[...], xn) + bgt_ref[...]

    call = _dot(xn, wc_ref[...])
    cq = call[:, :Q_LORA]
    ckv = call[:, Q_LORA:Q_LORA + KV_LORA]
    kra = call[:, Q_LORA + KV_LORA:Q_LORA + KV_LORA + LANES]
    krb = call[:, Q_LORA + KV_LORA + LANES:]
    cos_t = cos_ref[...]
    sin_t = sin_ref[...]
    scale = SOFTMAX_SCALE * LOG2E

    cqn = _rms(cq, qg_ref[...]).astype(BF16)
    qn = _dot(cqn, wuqn_ref[...]).astype(BF16)
    qra = _dot(cqn, wuqa_ref[...])
    qrb = _dot(cqn, wuqb_ref[...])
    for h in range(MLA_HEADS):
        sl = slice(h * LANES, (h + 1) * LANES)
        q_lat = _dot(qn[:, sl], wukt_ref[h])
        q_rope = qra[:, sl] * cos_t + qrb[:, sl] * sin_t
        qa_ref[0, h, :, 0:LANES] = (q_lat * scale).astype(BF16)
        qa_ref[0, h, :, LANES:ATT_DIM] = (q_rope * scale).astype(BF16)

    ka_ref[0, :, 0:LANES] = _rms(ckv, kvg_ref[...]).astype(BF16)
    k_rope = kra * cos_t + krb * sin_t
    lane = lax.broadcasted_iota(jnp.int32, k_rope.shape, 1)
    ka_ref[0, :, LANES:ATT_DIM] = jnp.where(lane >= QK_ROPE, 1.0, k_rope).astype(BF16)


def _in_proj(x, g_mix, w, tm):
    b, s, d = x.shape
    n_tiles = s // tm
    hb = tm // HALO_ROWS
    n_halo = s // HALO_ROWS

    def const(shape):
        return pl.BlockSpec(shape, lambda bi, ti: (0,) * len(shape))

    in_specs = [
        pl.BlockSpec((1, tm, d), lambda bi, ti: (bi, ti, 0)),
        pl.BlockSpec((1, HALO_ROWS, d), lambda bi, ti: (bi, jnp.maximum(ti * hb - 1, 0), 0)),
        pl.BlockSpec((1, HALO_ROWS, d), lambda bi, ti: (bi, jnp.minimum((ti + 1) * hb, n_halo - 1), 0)),
        const((1, d)),
        const(w["wqk"].shape), const(w["wv"].shape), const(w["wo"].shape),
        const(w["wgt"].shape), const(w["bgt"].shape),
        const(w["wc"].shape), const(w["cw"].shape), const(w["cb"].shape),
        const(w["qg"].shape), const(w["wuqn"].shape), const(w["wuqa"].shape), const(w["wuqb"].shape),
        const(w["kvg"].shape), const(w["wukt"].shape),
        pl.BlockSpec((tm, LANES), lambda bi, ti: (ti, 0)),
        pl.BlockSpec((tm, LANES), lambda bi, ti: (ti, 0)),
    ]
    row_spec = lambda width: pl.BlockSpec((1, tm, width), lambda bi, ti: (bi, ti, 0))
    out_specs = [
        row_spec(MLSTM_WIDTH),
        pl.BlockSpec((1, MLSTM_WIDTH, tm), lambda bi, ti: (bi, 0, ti)),
        row_spec(MLSTM_WIDTH), row_spec(MLSTM_WIDTH),
        pl.BlockSpec((1, N_GATES, tm), lambda bi, ti: (bi, 0, ti)),
        pl.BlockSpec((1, MLA_HEADS, tm, ATT_DIM), lambda bi, ti: (bi, 0, ti, 0)),
        row_spec(ATT_DIM),
    ]
    out_shape = [
        jax.ShapeDtypeStruct((b, s, MLSTM_WIDTH), BF16),
        jax.ShapeDtypeStruct((b, MLSTM_WIDTH, s), BF16),
        jax.ShapeDtypeStruct((b, s, MLSTM_WIDTH), BF16),
        jax.ShapeDtypeStruct((b, s, MLSTM_WIDTH), BF16),
        jax.ShapeDtypeStruct((b, N_GATES, s), F32),
        jax.ShapeDtypeStruct((b, MLA_HEADS, s, ATT_DIM), BF16),
        jax.ShapeDtypeStruct((b, s, ATT_DIM), BF16),
    ]
    return pl.pallas_call(
        functools.partial(_in_proj_kernel, n_tiles=n_tiles),
        grid=(b, n_tiles),
        in_specs=in_specs,
        out_specs=out_specs,
        out_shape=out_shape,
        compiler_params=pltpu.CompilerParams(
            dimension_semantics=("parallel", "parallel"), vmem_limit_bytes=VMEM_LIMIT_BYTES),
        name="in_proj",
    )(x, x, x, g_mix, w["wqk"], w["wv"], w["wo"], w["wgt"], w["bgt"], w["wc"],
      w["cw"], w["cb"], w["qg"], w["wuqn"], w["wuqa"], w["wuqb"], w["kvg"], w["wukt"],
      w["cos"], w["sin"])


def _log_sigmoid(x):
    return jnp.minimum(x, 0.0) - jnp.log1p(jnp.exp(-jnp.abs(x)))


def _split_bf16(x):
    hi = x.astype(BF16)
    lo = (x - hi.astype(F32)).astype(BF16)
    return hi, lo


def _mlstm_direction(q_ref, kt_ref, v_ref, gt_ref, o_ref, c_ref, m_ref, *, reverse):
    L = MLSTM_CHUNK
    t_idx = lax.broadcasted_iota(jnp.int32, (L, L), 0)
    s_idx = lax.broadcasted_iota(jnp.int32, (L, L), 1)
    lower = s_idx <= t_idx
    upper = s_idx >= t_idx
    sees = upper if reverse else lower
    sees_bf = jnp.where(sees, 1.0, 0.0).astype(BF16)
    sees_t_bf = jnp.where(lower if reverse else upper, 1.0, 0.0).astype(BF16)
    sees2_bf = jnp.concatenate([sees_bf, sees_bf], axis=1)
    kind = 2 if reverse else 0

    grow = gt_ref[0]
    hi, lo = _split_bf16(_log_sigmoid(grow))
    cum_row = _dot(hi, sees_t_bf) + _dot(lo, sees_t_bf)
    hi = hi.astype(F32)
    lo = lo.astype(F32)

    ones_aug = jnp.ones((L, MLSTM_HEAD_DIM), BF16)
    last = 0 if reverse else L - 1
    units = []
    for h in range(MLSTM_HEADS):
        ci = kind * MLSTM_HEADS + h
        cf = (kind + 1) * MLSTM_HEADS + h
        sl = slice(h * MLSTM_HEAD_DIM, (h + 1) * MLSTM_HEAD_DIM)
        u = _Unit()
        u.h, u.sl, u.sees = h, sl, sees
        u.o_ref, u.c_ref, u.m_ref = o_ref, c_ref, m_ref
        u.g_r = cum_row[cf:cf + 1, :]
        u.logi_r = grow[ci:ci + 1, :]
        u.g_last = u.g_r[:, last:last + 1]
        u.m_prev = m_ref[h]
        f_rep = jnp.concatenate([jnp.broadcast_to(hi[cf:cf + 1, :], (LANES, L)),
                                 jnp.broadcast_to(lo[cf:cf + 1, :], (LANES, L))], axis=1).astype(BF16)
        u.g_c = _dot_nt(sees2_bf, f_rep)
        u.qh = q_ref[0, :, sl]
        u.kht = kt_ref[0, sl, :]
        u.v_aug = jnp.concatenate([v_ref[0, :, sl], ones_aug], axis=1)
        units.append(u)
    return units


class _Unit:
    pass


def _mlstm_kernel(qf_ref, ktf_ref, vf_ref, gtf_ref, qb_ref, ktb_ref, vb_ref, gtb_ref,
                  hf_ref, hb_ref, cf_ref, mf_ref, cb_ref, mb_ref):
    @pl.when(pl.program_id(1) == 0)
    def _():
        cf_ref[...] = jnp.zeros_like(cf_ref)
        mf_ref[...] = jnp.zeros_like(mf_ref)
        cb_ref[...] = jnp.zeros_like(cb_ref)
        mb_ref[...] = jnp.zeros_like(mb_ref)

    units = (_mlstm_direction(qf_ref, ktf_ref, vf_ref, gtf_ref, hf_ref, cf_ref, mf_ref, reverse=False)
             + _mlstm_direction(qb_ref, ktb_ref, vb_ref, gtb_ref, hb_ref, cb_ref, mb_ref, reverse=True))
    for u in units:
        u.qk = _dot(u.qh, u.kht)
        u.c_aug = u.c_ref[u.h]
    for u in units:
        u.row = u.logi_r - u.g_r
        a_r = u.g_last + u.row
        m_loc = jnp.max(a_r, axis=1, keepdims=True)
        u.m_new = jnp.maximum(u.g_last + u.m_prev, m_loc)
        u.s_old = jnp.exp(u.g_last + u.m_prev - u.m_new)
        w_r = jnp.exp(a_r - m_loc) * jnp.exp(m_loc - u.m_new)
        kw = (u.kht.astype(F32) * w_r).astype(BF16)
        u.upd = _dot(kw, u.v_aug)
    for u in units:
        d = jnp.where(u.sees, u.g_c + u.row, -jnp.inf)
        b_inter = u.g_c + u.m_prev
        u.m_t = jnp.maximum(jnp.max(d, axis=1, keepdims=True), b_inter)
        u.scores = (u.qk * jnp.exp(d - u.m_t)).astype(BF16)
        u.inter = jnp.exp(b_inter - u.m_t).astype(BF16)
    for u in units:
        lhs = jnp.concatenate([u.scores, u.inter * u.qh], axis=1)
        rhs = jnp.concatenate([u.v_aug, u.c_aug.astype(BF16)], axis=0)
        tot = _dot(lhs, rhs)
        num = tot[:, :MLSTM_HEAD_DIM]
        den = tot[:, MLSTM_HEAD_DIM:]
        u.o_ref[0, :, u.sl] = (num / jnp.maximum(jnp.abs(den), jnp.exp(-u.m_t))).astype(u.o_ref.dtype)
    for u in units:
        u.c_ref[u.h] = jnp.concatenate([u.s_old, u.s_old], axis=1) * u.c_aug + u.upd
        u.m_ref[u.h] = u.m_new


def _mlstm(qm, kt, vm, gt):
    b, s, _ = qm.shape
    L = MLSTM_CHUNK
    nc = s // L
    rows = lambda ci: pl.BlockSpec((1, L, MLSTM_WIDTH), lambda bi, c: (bi, ci(c), 0))
    cols = lambda height, ci: pl.BlockSpec((1, height, L), lambda bi, c: (bi, 0, ci(c)))
    fwd = lambda c: c
    bwd = lambda c: nc - 1 - c
    in_specs = [rows(fwd), cols(MLSTM_WIDTH, fwd), rows(fwd), cols(N_GATES, fwd),
                rows(bwd), cols(MLSTM_WIDTH, bwd), rows(bwd), cols(N_GATES, bwd)]
    state = pltpu.VMEM((MLSTM_HEADS, MLSTM_HEAD_DIM, 2 * MLSTM_HEAD_DIM), F32)
    stab = pltpu.VMEM((MLSTM_HEADS, 1, LANES), F32)
    return pl.pallas_call(
        _mlstm_kernel,
        grid=(b, nc),
        in_specs=in_specs,
        out_specs=[rows(fwd), rows(bwd)],
        out_shape=[jax.ShapeDtypeStruct((b, s, MLSTM_WIDTH), BF16)] * 2,
        scratch_shapes=[state, stab, state, stab],
        compiler_params=pltpu.CompilerParams(
            dimension_semantics=("parallel", "arbitrary"), vmem_limit_bytes=VMEM_LIMIT_BYTES),
        name="mlstm",
    )(qm, kt, vm, gt, qm, kt, vm, gt)


def _mla_attn_kernel(q_ref, k_ref, wuv_ref, hf_ref, hb_ref, og_ref, x_ref, ng_ref, wout_ref,
                     o_ref, m_ref, alpha_ref, p_ref, acc_ref, y_ref, *, tk, rows):
    heads, tq, _ = q_ref.shape[1:]
    n_kv = k_ref.shape[1] // tk
    chains = [(h, r) for h in range(heads) for r in range(tq // rows)]
    m_ref[...] = jnp.full_like(m_ref, -jnp.inf)
    acc_ref[...] = jnp.zeros_like(acc_ref)

    def kv_block(j):
        start = j * tk if isinstance(j, int) else pl.multiple_of(j * tk, tk)
        return k_ref[0, pl.ds(start, tk), :]

    def probs(c, kv, slot):
        h, r = chains[c]
        s = _dot_nt(q_ref[0, h, r * rows:(r + 1) * rows, :], kv)
        m_prev = m_ref[c]
        m_new = jnp.maximum(m_prev, jnp.max(s, axis=1, keepdims=True))
        alpha_ref[slot, c] = jnp.exp2(m_prev - m_new)
        p_ref[slot, c] = jnp.exp2(s - jnp.concatenate([m_new] * (tk // LANES), axis=1)).astype(BF16)
        m_ref[c] = m_new

    def accumulate(c, kv, slot):
        alpha = alpha_ref[slot, c]
        acc_ref[c] = acc_ref[c] * jnp.concatenate([alpha, alpha], axis=1) + _dot(p_ref[slot, c], kv)

    def stage(j_acc, slot_acc, j_probs, slot_probs):
        kv_acc = None if j_acc is None else kv_block(j_acc)
        kv_probs = None if j_probs is None else kv_block(j_probs)
        for c in range(len(chains)):
            if j_acc is not None:
                accumulate(c, kv_acc, slot_acc)
            if j_probs is not None:
                probs(c, kv_probs, slot_probs)

    stage(None, None, 0, 0)
    n_pairs = (n_kv - 1) // 2

    def body(i, carry):
        j = 2 * i + 1
        stage(j - 1, 0, j, 1)
        stage(j, 1, j + 1, 0)
        return carry

    lax.fori_loop(0, n_pairs, body, 0)
    if (n_kv - 1) % 2:
        stage(n_kv - 2, 0, n_kv - 1, 1)
        stage(n_kv - 1, 1, None, None)
    else:
        stage(n_kv - 1, 0, None, None)

    h_ml = hf_ref[0].astype(F32) + hb_ref[0].astype(F32)
    ng = ng_ref[...]
    for hd in range(MLSTM_HEADS):
        sl = slice(hd * MLSTM_HEAD_DIM, (hd + 1) * MLSTM_HEAD_DIM)
        y_ref[:, sl] = (og_ref[0, :, sl].astype(F32) * _rms(h_ml[:, sl], ng[:, sl])).astype(BF16)
    for c, (h, r) in enumerate(chains):
        acc = acc_ref[c]
        o_lat = (acc[:, :KV_LORA] / acc[:, ATT_DIM - 1:ATT_DIM]).astype(BF16)
        lo = MLSTM_WIDTH + h * V_HEAD
        y_ref[r * rows:(r + 1) * rows, lo:lo + V_HEAD] = _dot(o_lat, wuv_ref[h]).astype(BF16)
    o_ref[0] = x_ref[0] + _dot(y_ref[...], wout_ref[...])


def _mla_attn_out(qa, ka, wuv, hf, hb, og, x, ng, wout, tq, tk):
    b, heads, s, _ = qa.shape
    d = x.shape[-1]
    rows = min(tq, ATT_CHAIN_ROWS)
    n_chains = heads * (tq // rows)
    row = lambda width: pl.BlockSpec((1, tq, width), lambda bi, qi: (bi, qi, 0))

    def resident(shape, index_map):
        return pl.BlockSpec(shape, index_map, pipeline_mode=pl.Buffered(1))

    return pl.pallas_call(
        functools.partial(_mla_attn_kernel, tk=tk, rows=rows),
        grid=(b, s // tq),
        in_specs=[
            pl.BlockSpec((1, heads, tq, ATT_DIM), lambda bi, qi: (bi, 0, qi, 0)),
            resident((1, s, ATT_DIM), lambda bi, qi: (bi, 0, 0)),
            resident(wuv.shape, lambda bi, qi: (0, 0, 0)),
            row(MLSTM_WIDTH), row(MLSTM_WIDTH), row(MLSTM_WIDTH), row(d),
            resident(ng.shape, lambda bi, qi: (0, 0)),
            resident(wout.shape, lambda bi, qi: (0, 0)),
        ],
        out_specs=row(d),
        out_shape=jax.ShapeDtypeStruct((b, s, d), F32),
        scratch_shapes=[pltpu.VMEM((n_chains, rows, LANES), F32), pltpu.VMEM((2, n_chains, rows, LANES), F32),
                        pltpu.VMEM((2, n_chains, rows, tk), BF16), pltpu.VMEM((n_chains, rows, ATT_DIM), F32),
                        pltpu.VMEM((tq, MLSTM_WIDTH + MLA_WIDTH), BF16)],
        compiler_params=pltpu.CompilerParams(
            dimension_semantics=("parallel", "parallel"), vmem_limit_bytes=VMEM_LIMIT_BYTES),
        name="mla_attn_out",
    )(qa, ka, wuv, hf, hb, og, x, ng, wout)


def _ffn_ple_kernel(h_ref, hp_ref, hn_ref, g_ref, wu_ref, cw_ref, cb_ref, wd_ref, p_ref, pg_ref,
                    wpg_ref, wpp_ref, ppg_ref, fg_ref, o_ref, a_ref, *, n_tiles, final_norm):
    tile = pl.program_id(1)
    d_ff = wd_ref.shape[0]
    h1 = h_ref[0]
    xa = _rms_rows_with_halo(h1, hp_ref[0], hn_ref[0], g_ref[...])

    for lo in range(0, d_ff, FF_CHUNK):
        g_sl = slice(lo, lo + FF_CHUNK)
        v_sl = slice(d_ff + lo, d_ff + lo + FF_CHUNK)
        gate = _conv3_rows(_dot(xa, wu_ref[:, g_sl]), cw_ref[:, g_sl], cb_ref[:, g_sl], tile, n_tiles)
        val = _conv3_rows(_dot(xa, wu_ref[:, v_sl]), cw_ref[:, v_sl], cb_ref[:, v_sl], tile, n_tiles)
        a_ref[:, g_sl] = (_silu(gate) * val).astype(BF16)

    h2 = h1 + _dot(a_ref[...], wd_ref[...])
    gate_p = 0.5 * jnp.tanh(0.5 * _dot(_rms(h2, pg_ref[...]).astype(BF16), wpg_ref[...])) + 0.5
    emb = _rms(_dot(p_ref[0].astype(BF16), wpp_ref[...]), ppg_ref[...])
    h3 = h2 + gate_p * emb
    o_ref[0] = _rms(h3, fg_ref[...]) if final_norm else h3


def _ffn_ple(h1, p, w, tm, final_norm):
    b, s, d = h1.shape
    d_ff = w["wd"].shape[0]
    n_tiles = s // tm
    hb = tm // HALO_ROWS
    n_halo = s // HALO_ROWS
    ple = p.shape[-1]

    def const(shape):
        return pl.BlockSpec(shape, lambda bi, ti: (0,) * len(shape), pipeline_mode=pl.Buffered(1))

    in_specs = [
        pl.BlockSpec((1, tm, d), lambda bi, ti: (bi, ti, 0)),
        pl.BlockSpec((1, HALO_ROWS, d), lambda bi, ti: (bi, jnp.maximum(ti * hb - 1, 0), 0)),
        pl.BlockSpec((1, HALO_ROWS, d), lambda bi, ti: (bi, jnp.minimum((ti + 1) * hb, n_halo - 1), 0)),
        const((1, d)), const((d, 2 * d_ff)), const((3, 2 * d_ff)), const((1, 2 * d_ff)), const((d_ff, d)),
        pl.BlockSpec((1, tm, ple), lambda bi, ti: (bi, ti, 0)),
        const((1, d)), const((d, d)), const((ple, d)), const((1, d)), const((1, d)),
    ]
    return pl.pallas_call(
        functools.partial(_ffn_ple_kernel, n_tiles=n_tiles, final_norm=final_norm),
        grid=(b, n_tiles),
        in_specs=in_specs,
        out_specs=pl.BlockSpec((1, tm, d), lambda bi, ti: (bi, ti, 0)),
        out_shape=jax.ShapeDtypeStruct((b, s, d), F32),
        scratch_shapes=[pltpu.VMEM((tm, d_ff), BF16)],
        compiler_params=pltpu.CompilerParams(
            dimension_semantics=("parallel", "parallel"), vmem_limit_bytes=VMEM_LIMIT_BYTES),
        name="ffn_ple",
    )(h1, h1, h1, w["g"], w["wu"], w["cw"], w["cb"], w["wd"], p,
      w["pg"], w["wpg"], w["wpp"], w["ppg"], w["fg"])


def _pick_tile(n, target):
    t = min(n, target)
    while n % t:
        t //= 2
    return t


def _rope_tables(s):
    pos = jnp.arange(s, dtype=F32)
    inv_freq = ROPE_THETA ** (-jnp.arange(0, QK_ROPE, 2, dtype=F32) / QK_ROPE)
    ang = pos[:, None] * inv_freq[None, :]
    cos, sin = jnp.cos(ang), jnp.sin(ang)
    pad = jnp.zeros((s, LANES - QK_ROPE), F32)
    return (jnp.concatenate([cos, cos, pad], axis=1), jnp.concatenate([-sin, sin, pad], axis=1))


def _swap_halves(w):
    half = w.shape[-1] // 2
    return jnp.concatenate([w[..., half:], w[..., :half]], axis=-1)


def _pad_lanes(w, width):
    return jnp.concatenate([w, jnp.zeros(w.shape[:-1] + (width - w.shape[-1],), w.dtype)], axis=-1)


def _in_proj_weights(w_in, b_gates, conv_w, conv_b, q_norm_g, w_uq, kv_norm_g, w_ukv, cos_t, sin_t):
    d = w_in.shape[0]
    o_q, o_v, o_o = 0, 2 * MLSTM_WIDTH, 3 * MLSTM_WIDTH
    o_g = 4 * MLSTM_WIDTH
    o_cq = o_g + N_GATES
    o_ckv = o_cq + Q_LORA
    o_kr = o_ckv + KV_LORA
    w_g = w_in[:, o_g:o_cq]
    w_kr = w_in[:, o_kr:o_kr + QK_ROPE]
    wc = jnp.concatenate([w_in[:, o_cq:o_kr], _pad_lanes(w_kr, LANES), _pad_lanes(_swap_halves(w_kr), LANES)],
                         axis=1)
    uq = w_uq.reshape(Q_LORA, MLA_HEADS, QK_NOPE + QK_ROPE)
    uq_rope = uq[:, :, QK_NOPE:]
    ukv = w_ukv.reshape(KV_LORA, MLA_HEADS, QK_NOPE + V_HEAD)
    return {
        "wqk": w_in[:, o_q:o_v].astype(BF16),
        "wv": w_in[:, o_v:o_o].astype(BF16),
        "wo": w_in[:, o_o:o_g].astype(BF16),
        "wgt": w_g.T.astype(BF16),
        "bgt": b_gates[:, None],
        "wc": wc.astype(BF16),
        "cw": conv_w, "cb": conv_b[None, :],
        "qg": q_norm_g[None, :],
        "wuqn": uq[:, :, :QK_NOPE].reshape(Q_LORA, MLA_HEADS * QK_NOPE).astype(BF16),
        "wuqa": _pad_lanes(uq_rope, LANES).reshape(Q_LORA, MLA_HEADS * LANES).astype(BF16),
        "wuqb": _pad_lanes(_swap_halves(uq_rope), LANES).reshape(Q_LORA, MLA_HEADS * LANES).astype(BF16),
        "kvg": kv_norm_g[None, :],
        "wukt": jnp.transpose(ukv[:, :, :QK_NOPE], (1, 2, 0)).astype(BF16),
        "cos": cos_t, "sin": sin_t,
    }, jnp.transpose(ukv[:, :, QK_NOPE:], (1, 0, 2)).astype(BF16)


def kernel(x, p, ln_mix_g, w_in, b_gates, conv_qk_w, conv_qk_b, mlstm_norm_g, q_norm_g, w_uq, kv_norm_g, w_ukv, w_out, ln_ffn_g, w_up, conv_ffn_w, conv_ffn_b, w_down, ple_norm_g, w_ple_gate, w_ple_proj, ple_post_g, final_g):
    depth = w_in.shape[0]
    _, s, _ = x.shape
    assert w_down.shape[1] % FF_CHUNK == 0
    tm = _pick_tile(s, 512)
    tq = _pick_tile(s, 512)
    tk = _pick_tile(s, 1024)
    cos_t, sin_t = _rope_tables(s)
    h = x
    for l in range(depth):
        w1, wuv = _in_proj_weights(w_in[l], b_gates[l], conv_qk_w[l], conv_qk_b[l], q_norm_g[l], w_uq[l],
                                   kv_norm_g[l], w_ukv[l], cos_t, sin_t)
        qm, kt, vm, og, gt, qa, ka = _in_proj(h, ln_mix_g[l][None, :], w1, _pick_tile(s, 1024))
        hf, hb = _mlstm(qm, kt, vm, gt)
        h1 = _mla_attn_out(qa, ka, wuv, hf, hb, og, h, mlstm_norm_g[l][None, :],
                           w_out[l].astype(BF16), tq, tk)
        w5 = {
            "g": ln_ffn_g[l][None, :], "wu": w_up[l].astype(BF16), "cw": conv_ffn_w[l],
            "cb": conv_ffn_b[l][None, :], "wd": w_down[l].astype(BF16),
            "pg": ple_norm_g[l][None, :], "wpg": w_ple_gate[l].astype(BF16),
            "wpp": w_ple_proj[l].astype(BF16), "ppg": ple_post_g[l][None, :], "fg": final_g[None, :],
        }
        h = _ffn_ple(h1, p[l], w5, tm, final_norm=(l == depth - 1))
    return h
```

```python
import functools
import math

import jax
import jax.numpy as jnp
from jax import lax
from jax.experimental import pallas as pl
from jax.experimental.pallas import tpu as pltpu

F32 = jnp.float32
BF16 = jnp.bfloat16

EPS = 1e-6
LANES = 128
HALO_ROWS = 8

MLSTM_HEADS = 4
MLSTM_HEAD_DIM = 128
MLSTM_WIDTH = MLSTM_HEADS * MLSTM_HEAD_DIM
MLSTM_CHUNK = 128
N_GATES = 4 * MLSTM_HEADS
MLA_HEADS = 4
QK_NOPE = 128
QK_ROPE = 64
V_HEAD = 128
Q_LORA = 256
KV_LORA = 128
MLA_WIDTH = MLA_HEADS * V_HEAD
ROPE_THETA = 10000.0
ATT_DIM = 2 * LANES
SOFTMAX_SCALE = (QK_NOPE + QK_ROPE) ** -0.5
LOG2E = math.log2(math.e)
ATT_CHAIN_ROWS = 256
FF_CHUNK = 512

VMEM_LIMIT_BYTES = 56 * 1024 * 1024


def _rms(xf, g):
    ms = jnp.mean(xf * xf, axis=-1, keepdims=True)
    return xf * lax.rsqrt(ms + EPS) * g


def _dot(a, b):
    return jnp.dot(a, b, preferred_element_type=F32)


def _dot_nt(a, b):
    return lax.dot_general(a, b, (((1,), (1,)), ((), ())), preferred_element_type=F32)


def _silu(x):
    h = 0.5 * x
    return h * jnp.tanh(h) + h


def _rms_rows_with_halo(x, x_prev, x_next, g):
    halo = jnp.concatenate([x_prev, x_next], axis=0)
    return jnp.concatenate([_rms(x, g).astype(BF16), _rms(halo, g).astype(BF16)], axis=0)


def _conv3_rows(u_all, cw, cb, tile, n_tiles):
    tm = u_all.shape[0] - 2 * HALO_ROWS
    u = u_all[:tm]
    prev_row = jnp.where(tile > 0, u_all[tm + HALO_ROWS - 1:tm + HALO_ROWS, :], 0.0)
    next_row = jnp.where(tile < n_tiles - 1, u_all[tm + HALO_ROWS:tm + HALO_ROWS + 1, :], 0.0)
    up = pltpu.roll(u, 1, 0)
    dn = pltpu.roll(u, tm - 1, 0)
    edge = lax.broadcasted_iota(jnp.int32, (HALO_ROWS, u.shape[1]), 0)
    up = jnp.concatenate([jnp.where(edge == 0, prev_row, up[:HALO_ROWS]), up[HALO_ROWS:]], axis=0)
    dn = jnp.concatenate([dn[:tm - HALO_ROWS],
                          jnp.where(edge == HALO_ROWS - 1, next_row, dn[tm - HALO_ROWS:])], axis=0)
    return up * cw[0:1, :] + u * cw[1:2, :] + dn * cw[2:3, :] + cb


def _in_proj_kernel(x_ref, xp_ref, xn_ref, g_ref, wqk_ref, wv_ref, wo_ref, wgt_ref,
                    bgt_ref, wc_ref, cw_ref, cb_ref, qg_ref, wuqn_ref, wuqa_ref,
                    wuqb_ref, kvg_ref, wukt_ref, cos_ref, sin_ref,
                    qm_ref, kt_ref, vm_ref, og_ref, gt_ref, qa_ref, ka_ref,
                    *, n_tiles):
    tile = pl.program_id(1)
    tm = x_ref.shape[1]
    xa = _rms_rows_with_halo(x_ref[0], xp_ref[0], xn_ref[0], g_ref[...])
    xn = xa[:tm]

    act = _silu(_conv3_rows(_dot(xa, wqk_ref[...]), cw_ref[...], cb_ref[...], tile, n_tiles))
    qm_ref[0] = act[:, :MLSTM_WIDTH].astype(BF16)
    kt_ref[0] = (act[:, MLSTM_WIDTH:] * (MLSTM_HEAD_DIM ** -0.5)).T.astype(BF16)
    vm_ref[0] = _dot(xn, wv_ref[...]).astype(BF16)
    og_ref[0] = (0.5 * jnp.tanh(0.5 * _dot(xn, wo_ref[...])) + 0.5).astype(BF16)

    gt_ref[0] = _dot_nt(wgt_ref[...], xn) + bgt_ref[...]

    call = _dot(xn, wc_ref[...])
    cq = call[:, :Q_LORA]
    ckv = call[:, Q_LORA:Q_LORA + KV_LORA]
    kra = call[:, Q_LORA + KV_LORA:Q_LORA + KV_LORA + LANES]
    krb = call[:, Q_LORA + KV_LORA + LANES:]
    cos_t = cos_ref[...]
    sin_t = sin_ref[...]
    scale = SOFTMAX_SCALE * LOG2E

    cqn = _rms(cq, qg_ref[...]).astype(BF16)
    qn = _dot(cqn, wuqn_ref[...]).astype(BF16)
    qra = _dot(cqn, wuqa_ref[...])
    qrb = _dot(cqn, wuqb_ref[...])
    for h in range(MLA_HEADS):
        sl = slice(h * LANES, (h + 1) * LANES)
        q_lat = _dot(qn[:, sl], wukt_ref[h])
        q_rope = qra[:, sl] * cos_t + qrb[:, sl] * sin_t
        qa_ref[0, h, :, 0:LANES] = (q_lat * scale).astype(BF16)
        qa_ref[0, h, :, LANES:ATT_DIM] = (q_rope * scale).astype(BF16)

    ka_ref[0, :, 0:LANES] = _rms(ckv, kvg_ref[...]).astype(BF16)
    k_rope = kra * cos_t + krb * sin_t
    lane = lax.broadcasted_iota(jnp.int32, k_rope.shape, 1)
    ka_ref[0, :, LANES:ATT_DIM] = jnp.where(lane >= QK_ROPE, 1.0, k_rope).astype(BF16)


def _in_proj(x, g_mix, w, tm):
    b, s, d = x.shape
    n_tiles = s // tm
    hb = tm // HALO_ROWS
    n_halo = s // HALO_ROWS

    def const(shape):
        return pl.BlockSpec(shape, lambda bi, ti: (0,) * len(shape))

    in_specs = [
        pl.BlockSpec((1, tm, d), lambda bi, ti: (bi, ti, 0)),
        pl.BlockSpec((1, HALO_ROWS, d), lambda bi, ti: (bi, jnp.maximum(ti * hb - 1, 0), 0)),
        pl.BlockSpec((1, HALO_ROWS, d), lambda bi, ti: (bi, jnp.minimum((ti + 1) * hb, n_halo - 1), 0)),
        const((1, d)),
        const(w["wqk"].shape), const(w["wv"].shape), const(w["wo"].shape),
        const(w["wgt"].shape), const(w["bgt"].shape),
        const(w["wc"].shape), const(w["cw"].shape), const(w["cb"].shape),
        const(w["qg"].shape), const(w["wuqn"].shape), const(w["wuqa"].shape), const(w["wuqb"].shape),
        const(w["kvg"].shape), const(w["wukt"].shape),
        pl.BlockSpec((tm, LANES), lambda bi, ti: (ti, 0)),
        pl.BlockSpec((tm, LANES), lambda bi, ti: (ti, 0)),
    ]
    row_spec = lambda width: pl.BlockSpec((1, tm, width), lambda bi, ti: (bi, ti, 0))
    out_specs = [
        row_spec(MLSTM_WIDTH),
        pl.BlockSpec((1, MLSTM_WIDTH, tm), lambda bi, ti: (bi, 0, ti)),
        row_spec(MLSTM_WIDTH), row_spec(MLSTM_WIDTH),
        pl.BlockSpec((1, N_GATES, tm), lambda bi, ti: (bi, 0, ti)),
        pl.BlockSpec((1, MLA_HEADS, tm, ATT_DIM), lambda bi, ti: (bi, 0, ti, 0)),
        row_spec(ATT_DIM),
    ]
    out_shape = [
        jax.ShapeDtypeStruct((b, s, MLSTM_WIDTH), BF16),
        jax.ShapeDtypeStruct((b, MLSTM_WIDTH, s), BF16),
        jax.ShapeDtypeStruct((b, s, MLSTM_WIDTH), BF16),
        jax.ShapeDtypeStruct((b, s, MLSTM_WIDTH), BF16),
        jax.ShapeDtypeStruct((b, N_GATES, s), F32),
        jax.ShapeDtypeStruct((b, MLA_HEADS, s, ATT_DIM), BF16),
        jax.ShapeDtypeStruct((b, s, ATT_DIM), BF16),
    ]
    return pl.pallas_call(
        functools.partial(_in_proj_kernel, n_tiles=n_tiles),
        grid=(b, n_tiles),
        in_specs=in_specs,
        out_specs=out_specs,
        out_shape=out_shape,
        compiler_params=pltpu.CompilerParams(
            dimension_semantics=("parallel", "parallel"), vmem_limit_bytes=VMEM_LIMIT_BYTES),
        name="in_proj",
    )(x, x, x, g_mix, w["wqk"], w["wv"], w["wo"], w["wgt"], w["bgt"], w["wc"],
      w["cw"], w["cb"], w["qg"], w["wuqn"], w["wuqa"], w["wuqb"], w["kvg"], w["wukt"],
      w["cos"], w["sin"])


def _log_sigmoid(x):
    return jnp.minimum(x, 0.0) - jnp.log1p(jnp.exp(-jnp.abs(x)))


def _split_bf16(x):
    hi = x.astype(BF16)
    lo = (x - hi.astype(F32)).astype(BF16)
    return hi, lo


def _mlstm_direction(q_ref, kt_ref, v_ref, gt_ref, o_ref, c_ref, m_ref, *, reverse):
    L = MLSTM_CHUNK
    t_idx = lax.broadcasted_iota(jnp.int32, (L, L), 0)
    s_idx = lax.broadcasted_iota(jnp.int32, (L, L), 1)
    lower = s_idx <= t_idx
    upper = s_idx >= t_idx
    sees = upper if reverse else lower
    sees_bf = jnp.where(sees, 1.0, 0.0).astype(BF16)
    sees_t_bf = jnp.where(lower if reverse else upper, 1.0, 0.0).astype(BF16)
    sees2_bf = jnp.concatenate([sees_bf, sees_bf], axis=1)
    kind = 2 if reverse else 0

    grow = gt_ref[0]
    hi, lo = _split_bf16(_log_sigmoid(grow))
    cum_row = _dot(hi, sees_t_bf) + _dot(lo, sees_t_bf)
    hi = hi.astype(F32)
    lo = lo.astype(F32)

    ones_aug = jnp.ones((L, MLSTM_HEAD_DIM), BF16)
    last = 0 if reverse else L - 1
    units = []
    for h in range(MLSTM_HEADS):
        ci = kind * MLSTM_HEADS + h
        cf = (kind + 1) * MLSTM_HEADS + h
        sl = slice(h * MLSTM_HEAD_DIM, (h + 1) * MLSTM_HEAD_DIM)
        u = _Unit()
        u.h, u.sl, u.sees = h, sl, sees
        u.o_ref, u.c_ref, u.m_ref = o_ref, c_ref, m_ref
        u.g_r = cum_row[cf:cf + 1, :]
        u.logi_r = grow[ci:ci + 1, :]
        u.g_last = u.g_r[:, last:last + 1]
        u.m_prev = m_ref[h]
        f_rep = jnp.concatenate([jnp.broadcast_to(hi[cf:cf + 1, :], (LANES, L)),
                                 jnp.broadcast_to(lo[cf:cf + 1, :], (LANES, L))], axis=1).astype(BF16)
        u.g_c = _dot_nt(sees2_bf, f_rep)
        u.qh = q_ref[0, :, sl]
        u.kht = kt_ref[0, sl, :]
        u.v_aug = jnp.concatenate([v_ref[0, :, sl], ones_aug], axis=1)
        units.append(u)
    return units


class _Unit:
    pass


def _mlstm_kernel(qf_ref, ktf_ref, vf_ref, gtf_ref, qb_ref, ktb_ref, vb_ref, gtb_ref,
                  hf_ref, hb_ref, cf_ref, mf_ref, cb_ref, mb_ref):
    @pl.when(pl.program_id(1) == 0)
    def _():
        cf_ref[...] = jnp.zeros_like(cf_ref)
        mf_ref[...] = jnp.zeros_like(mf_ref)
        cb_ref[...] = jnp.zeros_like(cb_ref)
        mb_ref[...] = jnp.zeros_like(mb_ref)

    units = (_mlstm_direction(qf_ref, ktf_ref, vf_ref, gtf_ref, hf_ref, cf_ref, mf_ref, reverse=False)
             + _mlstm_direction(qb_ref, ktb_ref, vb_ref, gtb_ref, hb_ref, cb_ref, mb_ref, reverse=True))
    for u in units:
        u.qk = _dot(u.qh, u.kht)
        u.c_aug = u.c_ref[u.h]
    for u in units:
        u.row = u.logi_r - u.g_r
        a_r = u.g_last + u.row
        m_loc = jnp.max(a_r, axis=1, keepdims=True)
        u.m_new = jnp.maximum(u.g_last + u.m_prev, m_loc)
        u.s_old = jnp.exp(u.g_last + u.m_prev - u.m_new)
        w_r = jnp.exp(a_r - m_loc) * jnp.exp(m_loc - u.m_new)
        kw = (u.kht.astype(F32) * w_r).astype(BF16)
        u.upd = _dot(kw, u.v_aug)
    for u in units:
        d = jnp.where(u.sees, u.g_c + u.row, -jnp.inf)
        b_inter = u.g_c + u.m_prev
        u.m_t = jnp.maximum(jnp.max(d, axis=1, keepdims=True), b_inter)
        u.scores = (u.qk * jnp.exp(d - u.m_t)).astype(BF16)
        u.inter = jnp.exp(b_inter - u.m_t).astype(BF16)
    for u in units:
        lhs = jnp.concatenate([u.scores, u.inter * u.qh], axis=1)
        rhs = jnp.concatenate([u.v_aug, u.c_aug.astype(BF16)], axis=0)
        tot = _dot(lhs, rhs)
        num = tot[:, :MLSTM_HEAD_DIM]
        den = tot[:, MLSTM_HEAD_DIM:]
        u.o_ref[0, :, u.sl] = (num / jnp.maximum(jnp.abs(den), jnp.exp(-u.m_t))).astype(u.o_ref.dtype)
    for u in units:
        u.c_ref[u.h] = jnp.concatenate([u.s_old, u.s_old], axis=1) * u.c_aug + u.upd
        u.m_ref[u.h] = u.m_new


def _mlstm(qm, kt, vm, gt):
    b, s, _ = qm.shape
    L = MLSTM_CHUNK
    nc = s // L
    rows = lambda ci: pl.BlockSpec((1, L, MLSTM_WIDTH), lambda bi, c: (bi, ci(c), 0))
    cols = lambda height, ci: pl.BlockSpec((1, height, L), lambda bi, c: (bi, 0, ci(c)))
    fwd = lambda c: c
    bwd = lambda c: nc - 1 - c
    in_specs = [rows(fwd), cols(MLSTM_WIDTH, fwd), rows(fwd), cols(N_GATES, fwd),
                rows(bwd), cols(MLSTM_WIDTH, bwd), rows(bwd), cols(N_GATES, bwd)]
    state = pltpu.VMEM((MLSTM_HEADS, MLSTM_HEAD_DIM, 2 * MLSTM_HEAD_DIM), F32)
    stab = pltpu.VMEM((MLSTM_HEADS, 1, LANES), F32)
    return pl.pallas_call(
        _mlstm_kernel,
        grid=(b, nc),
        in_specs=in_specs,
        out_specs=[rows(fwd), rows(bwd)],
        out_shape=[jax.ShapeDtypeStruct((b, s, MLSTM_WIDTH), BF16)] * 2,
        scratch_shapes=[state, stab, state, stab],
        compiler_params=pltpu.CompilerParams(
            dimension_semantics=("parallel", "arbitrary"), vmem_limit_bytes=VMEM_LIMIT_BYTES),
        name="mlstm",
    )(qm, kt, vm, gt, qm, kt, vm, gt)


def _mla_attn_kernel(q_ref, k_ref, wuv_ref, hf_ref, hb_ref, og_ref, x_ref, ng_ref, wout_ref,
                     o_ref, m_ref, alpha_ref, p_ref, acc_ref, y_ref, *, tk, rows):
    heads, tq, _ = q_ref.shape[1:]
    n_kv = k_ref.shape[1] // tk
    chains = [(h, r) for h in range(heads) for r in range(tq // rows)]
    m_ref[...] = jnp.full_like(m_ref, -jnp.inf)
    acc_ref[...] = jnp.zeros_like(acc_ref)

    def kv_block(j):
        start = j * tk if isinstance(j, int) else pl.multiple_of(j * tk, tk)
        return k_ref[0, pl.ds(start, tk), :]

    def probs(c, kv, slot):
        h, r = chains[c]
        s = _dot_nt(q_ref[0, h, r * rows:(r + 1) * rows, :], kv)
        m_prev = m_ref[c]
        m_new = jnp.maximum(m_prev, jnp.max(s, axis=1, keepdims=True))
        alpha_ref[slot, c] = jnp.exp2(m_prev - m_new)
        p_ref[slot, c] = jnp.exp2(s - jnp.concatenate([m_new] * (tk // LANES), axis=1)).astype(BF16)
        m_ref[c] = m_new

    def accumulate(c, kv, slot):
        alpha = alpha_ref[slot, c]
        acc_ref[c] = acc_ref[c] * jnp.concatenate([alpha, alpha], axis=1) + _dot(p_ref[slot, c], kv)

    def stage(j_acc, slot_acc, j_probs, slot_probs):
        kv_acc = None if j_acc is None else kv_block(j_acc)
        kv_probs = None if j_probs is None else kv_block(j_probs)
        for c in range(len(chains)):
            if j_acc is not None:
                accumulate(c, kv_acc, slot_acc)
            if j_probs is not None:
                probs(c, kv_probs, slot_probs)

    stage(None, None, 0, 0)
    n_pairs = (n_kv - 1) // 2

    def body(i, carry):
        j = 2 * i + 1
        stage(j - 1, 0, j, 1)
        stage(j, 1, j + 1, 0)
        return carry

    lax.fori_loop(0, n_pairs, body, 0)
    if (n_kv - 1) % 2:
        stage(n_kv - 2, 0, n_kv - 1, 1)
        stage(n_kv - 1, 1, None, None)
    else:
        stage(n_kv - 1, 0, None, None)

    h_ml = hf_ref[0].astype(F32) + hb_ref[0].astype(F32)
    ng = ng_ref[...]
    for hd in range(MLSTM_HEADS):
        sl = slice(hd * MLSTM_HEAD_DIM, (hd + 1) * MLSTM_HEAD_DIM)
        y_ref[:, sl] = (og_ref[0, :, sl].astype(F32) * _rms(h_ml[:, sl], ng[:, sl])).astype(BF16)
    for c, (h, r) in enumerate(chains):
        acc = acc_ref[c]
        o_lat = (acc[:, :KV_LORA] / acc[:, ATT_DIM - 1:ATT_DIM]).astype(BF16)
        lo = MLSTM_WIDTH + h * V_HEAD
        y_ref[r * rows:(r + 1) * rows, lo:lo + V_HEAD] = _dot(o_lat, wuv_ref[h]).astype(BF16)
    o_ref[0] = x_ref[0] + _dot(y_ref[...], wout_ref[...])


def _mla_attn_out(qa, ka, wuv, hf, hb, og, x, ng, wout, tq, tk):
    b, heads, s, _ = qa.shape
    d = x.shape[-1]
    rows = min(tq, ATT_CHAIN_ROWS)
    n_chains = heads * (tq // rows)
    row = lambda width: pl.BlockSpec((1, tq, width), lambda bi, qi: (bi, qi, 0))

    def resident(shape, index_map):
        return pl.BlockSpec(shape, index_map, pipeline_mode=pl.Buffered(1))

    return pl.pallas_call(
        functools.partial(_mla_attn_kernel, tk=tk, rows=rows),
        grid=(b, s // tq),
        in_specs=[
            pl.BlockSpec((1, heads, tq, ATT_DIM), lambda bi, qi: (bi, 0, qi, 0)),
            resident((1, s, ATT_DIM), lambda bi, qi: (bi, 0, 0)),
            resident(wuv.shape, lambda bi, qi: (0, 0, 0)),
            row(MLSTM_WIDTH), row(MLSTM_WIDTH), row(MLSTM_WIDTH), row(d),
            resident(ng.shape, lambda bi, qi: (0, 0)),
            resident(wout.shape, lambda bi, qi: (0, 0)),
        ],
        out_specs=row(d),
        out_shape=jax.ShapeDtypeStruct((b, s, d), F32),
        scratch_shapes=[pltpu.VMEM((n_chains, rows, LANES), F32), pltpu.VMEM((2, n_chains, rows, LANES), F32),
                        pltpu.VMEM((2, n_chains, rows, tk), BF16), pltpu.VMEM((n_chains, rows, ATT_DIM), F32),
                        pltpu.VMEM((tq, MLSTM_WIDTH + MLA_WIDTH), BF16)],
        compiler_params=pltpu.CompilerParams(
            dimension_semantics=("parallel", "parallel"), vmem_limit_bytes=VMEM_LIMIT_BYTES),
        name="mla_attn_out",
    )(qa, ka, wuv, hf, hb, og, x, ng, wout)


def _ffn_ple_kernel(h_ref, hp_ref, hn_ref, g_ref, wu_ref, cw_ref, cb_ref, wd_ref, p_ref, pg_ref,
                    wpg_ref, wpp_ref, ppg_ref, fg_ref, o_ref, a_ref, *, n_tiles, final_norm):
    tile = pl.program_id(1)
    d_ff = wd_ref.shape[0]
    h1 = h_ref[0]
    xa = _rms_rows_with_halo(h1, hp_ref[0], hn_ref[0], g_ref[...])

    for lo in range(0, d_ff, FF_CHUNK):
        hi = min(lo + FF_CHUNK, d_ff)
        g_sl = slice(lo, hi)
        v_sl = slice(d_ff + lo, d_ff + hi)
        gate = _conv3_rows(_dot(xa, wu_ref[:, g_sl]), cw_ref[:, g_sl], cb_ref[:, g_sl], tile, n_tiles)
        val = _conv3_rows(_dot(xa, wu_ref[:, v_sl]), cw_ref[:, v_sl], cb_ref[:, v_sl], tile, n_tiles)
        a_ref[:, g_sl] = (_silu(gate) * val).astype(BF16)

    h2 = h1 + _dot(a_ref[...], wd_ref[...])
    gate_p = 0.5 * jnp.tanh(0.5 * _dot(_rms(h2, pg_ref[...]).astype(BF16), wpg_ref[...])) + 0.5
    emb = _rms(_dot(p_ref[0].astype(BF16), wpp_ref[...]), ppg_ref[...])
    h3 = h2 + gate_p * emb
    o_ref[0] = _rms(h3, fg_ref[...]) if final_norm else h3


def _ffn_ple(h1, p, w, tm, final_norm):
    b, s, d = h1.shape
    d_ff = w["wd"].shape[0]
    n_tiles = s // tm
    hb = tm // HALO_ROWS
    n_halo = s // HALO_ROWS
    ple = p.shape[-1]

    def const(shape):
        return pl.BlockSpec(shape, lambda bi, ti: (0,) * len(shape), pipeline_mode=pl.Buffered(1))

    in_specs = [
        pl.BlockSpec((1, tm, d), lambda bi, ti: (bi, ti, 0)),
        pl.BlockSpec((1, HALO_ROWS, d), lambda bi, ti: (bi, jnp.maximum(ti * hb - 1, 0), 0)),
        pl.BlockSpec((1, HALO_ROWS, d), lambda bi, ti: (bi, jnp.minimum((ti + 1) * hb, n_halo - 1), 0)),
        const((1, d)), const((d, 2 * d_ff)), const((3, 2 * d_ff)), const((1, 2 * d_ff)), const((d_ff, d)),
        pl.BlockSpec((1, tm, ple), lambda bi, ti: (bi, ti, 0)),
        const((1, d)), const((d, d)), const((ple, d)), const((1, d)), const((1, d)),
    ]
    return pl.pallas_call(
        functools.partial(_ffn_ple_kernel, n_tiles=n_tiles, final_norm=final_norm),
        grid=(b, n_tiles),
        in_specs=in_specs,
        out_specs=pl.BlockSpec((1, tm, d), lambda bi, ti: (bi, ti, 0)),
        out_shape=jax.ShapeDtypeStruct((b, s, d), F32),
        scratch_shapes=[pltpu.VMEM((tm, d_ff), BF16)],
        compiler_params=pltpu.CompilerParams(
            dimension_semantics=("parallel", "parallel"), vmem_limit_bytes=VMEM_LIMIT_BYTES),
        name="ffn_ple",
    )(h1, h1, h1, w["g"], w["wu"], w["cw"], w["cb"], w["wd"], p,
      w["pg"], w["wpg"], w["wpp"], w["ppg"], w["fg"])


def _pick_tile(n, target):
    t = min(n, target)
    while n % t:
        t //= 2
    return t


def _rope_tables(s):
    pos = jnp.arange(s, dtype=F32)
    inv_freq = ROPE_THETA ** (-jnp.arange(0, QK_ROPE, 2, dtype=F32) / QK_ROPE)
    ang = pos[:, None] * inv_freq[None, :]
    cos, sin = jnp.cos(ang), jnp.sin(ang)
    pad = jnp.zeros((s, LANES - QK_ROPE), F32)
    return (jnp.concatenate([cos, cos, pad], axis=1), jnp.concatenate([-sin, sin, pad], axis=1))


def _swap_halves(w):
    half = w.shape[-1] // 2
    return jnp.concatenate([w[..., half:], w[..., :half]], axis=-1)


def _pad_lanes(w, width):
    return jnp.concatenate([w, jnp.zeros(w.shape[:-1] + (width - w.shape[-1],), w.dtype)], axis=-1)


def _in_proj_weights(w_in, b_gates, conv_w, conv_b, q_norm_g, w_uq, kv_norm_g, w_ukv, cos_t, sin_t):
    d = w_in.shape[0]
    o_q, o_v, o_o = 0, 2 * MLSTM_WIDTH, 3 * MLSTM_WIDTH
    o_g = 4 * MLSTM_WIDTH
    o_cq = o_g + N_GATES
    o_ckv = o_cq + Q_LORA
    o_kr = o_ckv + KV_LORA
    w_g = w_in[:, o_g:o_cq]
    w_kr = w_in[:, o_kr:o_kr + QK_ROPE]
    wc = jnp.concatenate([w_in[:, o_cq:o_kr], _pad_lanes(w_kr, LANES), _pad_lanes(_swap_halves(w_kr), LANES)],
                         axis=1)
    uq = w_uq.reshape(Q_LORA, MLA_HEADS, QK_NOPE + QK_ROPE)
    uq_rope = uq[:, :, QK_NOPE:]
    ukv = w_ukv.reshape(KV_LORA, MLA_HEADS, QK_NOPE + V_HEAD)
    return {
        "wqk": w_in[:, o_q:o_v].astype(BF16),
        "wv": w_in[:, o_v:o_o].astype(BF16),
        "wo": w_in[:, o_o:o_g].astype(BF16),
        "wgt": w_g.T.astype(BF16),
        "bgt": b_gates[:, None],
        "wc": wc.astype(BF16),
        "cw": conv_w, "cb": conv_b[None, :],
        "qg": q_norm_g[None, :],
        "wuqn": uq[:, :, :QK_NOPE].reshape(Q_LORA, MLA_HEADS * QK_NOPE).astype(BF16),
        "wuqa": _pad_lanes(uq_rope, LANES).reshape(Q_LORA, MLA_HEADS * LANES).astype(BF16),
        "wuqb": _pad_lanes(_swap_halves(uq_rope), LANES).reshape(Q_LORA, MLA_HEADS * LANES).astype(BF16),
        "kvg": kv_norm_g[None, :],
        "wukt": jnp.transpose(ukv[:, :, :QK_NOPE], (1, 2, 0)).astype(BF16),
        "cos": cos_t, "sin": sin_t,
    }, jnp.transpose(ukv[:, :, QK_NOPE:], (1, 0, 2)).astype(BF16)


def kernel(x, p, ln_mix_g, w_in, b_gates, conv_qk_w, conv_qk_b, mlstm_norm_g, q_norm_g, w_uq, kv_norm_g, w_ukv, w_out, ln_ffn_g, w_up, conv_ffn_w, conv_ffn_b, w_down, ple_norm_g, w_ple_gate, w_ple_proj, ple_post_g, final_g):
    depth = w_in.shape[0]
    _, s, _ = x.shape
    assert w_down.shape[1] % LANES == 0
    tm = _pick_tile(s, 512)
    tq = _pick_tile(s, 512)
    tk = _pick_tile(s, 1024)
    cos_t, sin_t = _rope_tables(s)
    h = x
    for l in range(depth):
        w1, wuv = _in_proj_weights(w_in[l], b_gates[l], conv_qk_w[l], conv_qk_b[l], q_norm_g[l], w_uq[l],
                                   kv_norm_g[l], w_ukv[l], cos_t, sin_t)
        qm, kt, vm, og, gt, qa, ka = _in_proj(h, ln_mix_g[l][None, :], w1, _pick_tile(s, 1024))
        hf, hb = _mlstm(qm, kt, vm, gt)
        h1 = _mla_attn_out(qa, ka, wuv, hf, hb, og, h, mlstm_norm_g[l][None, :],
                           w_out[l].astype(BF16), tq, tk)
        w5 = {
            "g": ln_ffn_g[l][None, :], "wu": w_up[l].astype(BF16), "cw": conv_ffn_w[l],
            "cb": conv_ffn_b[l][None, :], "wd": w_down[l].astype(BF16),
            "pg": ple_norm_g[l][None, :], "wpg": w_ple_gate[l].astype(BF16),
            "wpp": w_ple_proj[l].astype(BF16), "ppg": ple_post_g[l][None, :], "fg": final_g[None, :],
        }
        h = _ffn_ple(h1, p[l], w5, tm, final_norm=(l == depth - 1))
    return h
```

```python
import functools
import math

import jax
import jax.numpy as jnp
from jax import lax
from jax.experimental import pallas as pl
from jax.experimental.pallas import tpu as pltpu

F32 = jnp.float32
BF16 = jnp.bfloat16

EPS = 1e-6
LANES = 128
HALO_ROWS = 8

MLSTM_HEADS = 4
MLSTM_HEAD_DIM = 128
MLSTM_WIDTH = MLSTM_HEADS * MLSTM_HEAD_DIM
MLSTM_CHUNK = 128
N_GATES = 4 * MLSTM_HEADS
MLA_HEADS = 4
QK_NOPE = 128
QK_ROPE = 64
V_HEAD = 128
Q_LORA = 256
KV_LORA = 128
MLA_WIDTH = MLA_HEADS * V_HEAD
ROPE_THETA = 10000.0
ATT_DIM = 2 * LANES
SOFTMAX_SCALE = (QK_NOPE + QK_ROPE) ** -0.5
LOG2E = math.log2(math.e)
ATT_CHAIN_ROWS = 256
FF_CHUNK = 256

VMEM_LIMIT_BYTES = 56 * 1024 * 1024


def _rms(xf, g):
    ms = jnp.mean(xf * xf, axis=-1, keepdims=True)
    return xf * lax.rsqrt(ms + EPS) * g


def _dot(a, b):
    return jnp.dot(a, b, preferred_element_type=F32)


def _dot_nt(a, b):
    return lax.dot_general(a, b, (((1,), (1,)), ((), ())), preferred_element_type=F32)


def _silu(x):
    h = 0.5 * x
    return h * jnp.tanh(h) + h


def _rms_rows_with_halo(x, x_prev, x_next, g):
    halo = jnp.concatenate([x_prev, x_next], axis=0)
    return jnp.concatenate([_rms(x, g).astype(BF16), _rms(halo, g).astype(BF16)], axis=0)


def _conv3_rows(u_all, cw, cb, tile, n_tiles):
    tm = u_all.shape[0] - 2 * HALO_ROWS
    u = u_all[:tm]
    prev_row = jnp.where(tile > 0, u_all[tm + HALO_ROWS - 1:tm + HALO_ROWS, :], 0.0)
    next_row = jnp.where(tile < n_tiles - 1, u_all[tm + HALO_ROWS:tm + HALO_ROWS + 1, :], 0.0)
    up = pltpu.roll(u, 1, 0)
    dn = pltpu.roll(u, tm - 1, 0)
    edge = lax.broadcasted_iota(jnp.int32, (HALO_ROWS, u.shape[1]), 0)
    up = jnp.concatenate([jnp.where(edge == 0, prev_row, up[:HALO_ROWS]), up[HALO_ROWS:]], axis=0)
    dn = jnp.concatenate([dn[:tm - HALO_ROWS],
                          jnp.where(edge == HALO_ROWS - 1, next_row, dn[tm - HALO_ROWS:])], axis=0)
    return up * cw[0:1, :] + u * cw[1:2, :] + dn * cw[2:3, :] + cb


def _in_proj_kernel(x_ref, xp_ref, xn_ref, g_ref, wqk_ref, wv_ref, wo_ref, wgt_ref,
                    bgt_ref, wc_ref, cw_ref, cb_ref, qg_ref, wuqn_ref, wuqa_ref,
                    wuqb_ref, kvg_ref, wukt_ref, cos_ref, sin_ref,
                    qm_ref, kt_ref, vm_ref, og_ref, gt_ref, qa_ref, ka_ref,
                    *, n_tiles):
    tile = pl.program_id(1)
    tm = x_ref.shape[1]
    xa = _rms_rows_with_halo(x_ref[0], xp_ref[0], xn_ref[0], g_ref[...])
    xn = xa[:tm]

    act = _silu(_conv3_rows(_dot(xa, wqk_ref[...]), cw_ref[...], cb_ref[...], tile, n_tiles))
    qm_ref[0] = act[:, :MLSTM_WIDTH].astype(BF16)
    kt_ref[0] = (act[:, MLSTM_WIDTH:] * (MLSTM_HEAD_DIM ** -0.5)).T.astype(BF16)
    vm_ref[0] = _dot(xn, wv_ref[...]).astype(BF16)
    og_ref[0] = (0.5 * jnp.tanh(0.5 * _dot(xn, wo_ref[...])) + 0.5).astype(BF16)

    gt_ref[0] = _dot_nt(wgt_ref[...], xn) + bgt_ref[...]

    call = _dot(xn, wc_ref[...])
    cq = call[:, :Q_LORA]
    ckv = call[:, Q_LORA:Q_LORA + KV_LORA]
    kra = call[:, Q_LORA + KV_LORA:Q_LORA + KV_LORA + LANES]
    krb = call[:, Q_LORA + KV_LORA + LANES:]
    cos_t = cos_ref[...]
    sin_t = sin_ref[...]
    scale = SOFTMAX_SCALE * LOG2E

    cqn = _rms(cq, qg_ref[...]).astype(BF16)
    qn = _dot(cqn, wuqn_ref[...]).astype(BF16)
    qra = _dot(cqn, wuqa_ref[...])
    qrb = _dot(cqn, wuqb_ref[...])
    for h in range(MLA_HEADS):
        sl = slice(h * LANES, (h + 1) * LANES)
        q_lat = _dot(qn[:, sl], wukt_ref[h])
        q_rope = qra[:, sl] * cos_t + qrb[:, sl] * sin_t
        qa_ref[0, h, :, 0:LANES] = (q_lat * scale).astype(BF16)
        qa_ref[0, h, :, LANES:ATT_DIM] = (q_rope * scale).astype(BF16)

    ka_ref[0, :, 0:LANES] = _rms(ckv, kvg_ref[...]).astype(BF16)
    k_rope = kra * cos_t + krb * sin_t
    lane = lax.broadcasted_iota(jnp.int32, k_rope.shape, 1)
    ka_ref[0, :, LANES:ATT_DIM] = jnp.where(lane >= QK_ROPE, 1.0, k_rope).astype(BF16)


def _in_proj(x, g_mix, w, tm):
    b, s, d = x.shape
    n_tiles = s // tm
    hb = tm // HALO_ROWS
    n_halo = s // HALO_ROWS

    def const(shape):
        return pl.BlockSpec(shape, lambda bi, ti: (0,) * len(shape))

    in_specs = [
        pl.BlockSpec((1, tm, d), lambda bi, ti: (bi, ti, 0)),
        pl.BlockSpec((1, HALO_ROWS, d), lambda bi, ti: (bi, jnp.maximum(ti * hb - 1, 0), 0)),
        pl.BlockSpec((1, HALO_ROWS, d), lambda bi, ti: (bi, jnp.minimum((ti + 1) * hb, n_halo - 1), 0)),
        const((1, d)),
        const(w["wqk"].shape), const(w["wv"].shape), const(w["wo"].shape),
        const(w["wgt"].shape), const(w["bgt"].shape),
        const(w["wc"].shape), const(w["cw"].shape), const(w["cb"].shape),
        const(w["qg"].shape), const(w["wuqn"].shape), const(w["wuqa"].shape), const(w["wuqb"].shape),
        const(w["kvg"].shape), const(w["wukt"].shape),
        pl.BlockSpec((tm, LANES), lambda bi, ti: (ti, 0)),
        pl.BlockSpec((tm, LANES), lambda bi, ti: (ti, 0)),
    ]
    row_spec = lambda width: pl.BlockSpec((1, tm, width), lambda bi, ti: (bi, ti, 0))
    out_specs = [
        row_spec(MLSTM_WIDTH),
        pl.BlockSpec((1, MLSTM_WIDTH, tm), lambda bi, ti: (bi, 0, ti)),
        row_spec(MLSTM_WIDTH), row_spec(MLSTM_WIDTH),
        pl.BlockSpec((1, N_GATES, tm), lambda bi, ti: (bi, 0, ti)),
        pl.BlockSpec((1, MLA_HEADS, tm, ATT_DIM), lambda bi, ti: (bi, 0, ti, 0)),
        row_spec(ATT_DIM),
    ]
    out_shape = [
        jax.ShapeDtypeStruct((b, s, MLSTM_WIDTH), BF16),
        jax.ShapeDtypeStruct((b, MLSTM_WIDTH, s), BF16),
        jax.ShapeDtypeStruct((b, s, MLSTM_WIDTH), BF16),
        jax.ShapeDtypeStruct((b, s, MLSTM_WIDTH), BF16),
        jax.ShapeDtypeStruct((b, N_GATES, s), F32),
        jax.ShapeDtypeStruct((b, MLA_HEADS, s, ATT_DIM), BF16),
        jax.ShapeDtypeStruct((b, s, ATT_DIM), BF16),
    ]
    return pl.pallas_call(
        functools.partial(_in_proj_kernel, n_tiles=n_tiles),
        grid=(b, n_tiles),
        in_specs=in_specs,
        out_specs=out_specs,
        out_shape=out_shape,
        compiler_params=pltpu.CompilerParams(
            dimension_semantics=("parallel", "parallel"), vmem_limit_bytes=VMEM_LIMIT_BYTES),
        name="in_proj",
    )(x, x, x, g_mix, w["wqk"], w["wv"], w["wo"], w["wgt"], w["bgt"], w["wc"],
      w["cw"], w["cb"], w["qg"], w["wuqn"], w["wuqa"], w["wuqb"], w["kvg"], w["wukt"],
      w["cos"], w["sin"])


def _log_sigmoid(x):
    return jnp.minimum(x, 0.0) - jnp.log1p(jnp.exp(-jnp.abs(x)))


def _split_bf16(x):
    hi = x.astype(BF16)
    lo = (x - hi.astype(F32)).astype(BF16)
    return hi, lo


def _mlstm_direction(q_ref, kt_ref, v_ref, gt_ref, o_ref, c_ref, m_ref, *, reverse):
    L = MLSTM_CHUNK
    t_idx = lax.broadcasted_iota(jnp.int32, (L, L), 0)
    s_idx = lax.broadcasted_iota(jnp.int32, (L, L), 1)
    lower = s_idx <= t_idx
    upper = s_idx >= t_idx
    sees = upper if reverse else lower
    sees_bf = jnp.where(sees, 1.0, 0.0).astype(BF16)
    sees_t_bf = jnp.where(lower if reverse else upper, 1.0, 0.0).astype(BF16)
    sees2_bf = jnp.concatenate([sees_bf, sees_bf], axis=1)
    kind = 2 if reverse else 0

    grow = gt_ref[0]
    hi, lo = _split_bf16(_log_sigmoid(grow))
    cum_row = _dot(hi, sees_t_bf) + _dot(lo, sees_t_bf)
    hi = hi.astype(F32)
    lo = lo.astype(F32)

    ones_aug = jnp.ones((L, MLSTM_HEAD_DIM), BF16)
    last = 0 if reverse else L - 1
    units = []
    for h in range(MLSTM_HEADS):
        ci = kind * MLSTM_HEADS + h
        cf = (kind + 1) * MLSTM_HEADS + h
        sl = slice(h * MLSTM_HEAD_DIM, (h + 1) * MLSTM_HEAD_DIM)
        u = _Unit()
        u.h, u.sl, u.sees = h, sl, sees
        u.o_ref, u.c_ref, u.m_ref = o_ref, c_ref, m_ref
        u.g_r = cum_row[cf:cf + 1, :]
        u.logi_r = grow[ci:ci + 1, :]
        u.g_last = u.g_r[:, last:last + 1]
        u.m_prev = m_ref[h]
        f_rep = jnp.concatenate([jnp.broadcast_to(hi[cf:cf + 1, :], (LANES, L)),
                                 jnp.broadcast_to(lo[cf:cf + 1, :], (LANES, L))], axis=1).astype(BF16)
        u.g_c = _dot_nt(sees2_bf, f_rep)
        u.qh = q_ref[0, :, sl]
        u.kht = kt_ref[0, sl, :]
        u.v_aug = jnp.concatenate([v_ref[0, :, sl], ones_aug], axis=1)
        units.append(u)
    return units


class _Unit:
    pass


def _mlstm_kernel(qf_ref, ktf_ref, vf_ref, gtf_ref, qb_ref, ktb_ref, vb_ref, gtb_ref,
                  hf_ref, hb_ref, cf_ref, mf_ref, cb_ref, mb_ref):
    @pl.when(pl.program_id(1) == 0)
    def _():
        cf_ref[...] = jnp.zeros_like(cf_ref)
        mf_ref[...] = jnp.zeros_like(mf_ref)
        cb_ref[...] = jnp.zeros_like(cb_ref)
        mb_ref[...] = jnp.zeros_like(mb_ref)

    units = (_mlstm_direction(qf_ref, ktf_ref, vf_ref, gtf_ref, hf_ref, cf_ref, mf_ref, reverse=False)
             + _mlstm_direction(qb_ref, ktb_ref, vb_ref, gtb_ref, hb_ref, cb_ref, mb_ref, reverse=True))
    for u in units:
        u.qk = _dot(u.qh, u.kht)
        u.c_aug = u.c_ref[u.h]
    for u in units:
        u.row = u.logi_r - u.g_r
        a_r = u.g_last + u.row
        m_loc = jnp.max(a_r, axis=1, keepdims=True)
        u.m_new = jnp.maximum(u.g_last + u.m_prev, m_loc)
        u.s_old = jnp.exp(u.g_last + u.m_prev - u.m_new)
        w_r = jnp.exp(a_r - m_loc) * jnp.exp(m_loc - u.m_new)
        kw = (u.kht.astype(F32) * w_r).astype(BF16)
        u.upd = _dot(kw, u.v_aug)
    for u in units:
        d = jnp.where(u.sees, u.g_c + u.row, -jnp.inf)
        b_inter = u.g_c + u.m_prev
        u.m_t = jnp.maximum(jnp.max(d, axis=1, keepdims=True), b_inter)
        u.scores = (u.qk * jnp.exp(d - u.m_t)).astype(BF16)
        u.inter = jnp.exp(b_inter - u.m_t).astype(BF16)
    for u in units:
        lhs = jnp.concatenate([u.scores, u.inter * u.qh], axis=1)
        rhs = jnp.concatenate([u.v_aug, u.c_aug.astype(BF16)], axis=0)
        tot = _dot(lhs, rhs)
        num = tot[:, :MLSTM_HEAD_DIM]
        den = tot[:, MLSTM_HEAD_DIM:]
        u.o_ref[0, :, u.sl] = (num / jnp.maximum(jnp.abs(den), jnp.exp(-u.m_t))).astype(u.o_ref.dtype)
    for u in units:
        u.c_ref[u.h] = jnp.concatenate([u.s_old, u.s_old], axis=1) * u.c_aug + u.upd
        u.m_ref[u.h] = u.m_new


def _mlstm(qm, kt, vm, gt):
    b, s, _ = qm.shape
    L = MLSTM_CHUNK
    nc = s // L
    rows = lambda ci: pl.BlockSpec((1, L, MLSTM_WIDTH), lambda bi, c: (bi, ci(c), 0))
    cols = lambda height, ci: pl.BlockSpec((1, height, L), lambda bi, c: (bi, 0, ci(c)))
    fwd = lambda c: c
    bwd = lambda c: nc - 1 - c
    in_specs = [rows(fwd), cols(MLSTM_WIDTH, fwd), rows(fwd), cols(N_GATES, fwd),
                rows(bwd), cols(MLSTM_WIDTH, bwd), rows(bwd), cols(N_GATES, bwd)]
    state = pltpu.VMEM((MLSTM_HEADS, MLSTM_HEAD_DIM, 2 * MLSTM_HEAD_DIM), F32)
    stab = pltpu.VMEM((MLSTM_HEADS, 1, LANES), F32)
    return pl.pallas_call(
        _mlstm_kernel,
        grid=(b, nc),
        in_specs=in_specs,
        out_specs=[rows(fwd), rows(bwd)],
        out_shape=[jax.ShapeDtypeStruct((b, s, MLSTM_WIDTH), BF16)] * 2,
        scratch_shapes=[state, stab, state, stab],
        compiler_params=pltpu.CompilerParams(
            dimension_semantics=("parallel", "arbitrary"), vmem_limit_bytes=VMEM_LIMIT_BYTES),
        name="mlstm",
    )(qm, kt, vm, gt, qm, kt, vm, gt)


def _mla_attn_kernel(q_ref, k_ref, wuv_ref, hf_ref, hb_ref, og_ref, x_ref, ng_ref, wout_ref,
                     o_ref, m_ref, alpha_ref, p_ref, acc_ref, y_ref, *, tk, rows):
    heads, tq, _ = q_ref.shape[1:]
    n_kv = k_ref.shape[1] // tk
    chains = [(h, r) for h in range(heads) for r in range(tq // rows)]
    m_ref[...] = jnp.full_like(m_ref, -jnp.inf)
    acc_ref[...] = jnp.zeros_like(acc_ref)

    def kv_block(j):
        start = j * tk if isinstance(j, int) else pl.multiple_of(j * tk, tk)
        return k_ref[0, pl.ds(start, tk), :]

    def probs(c, kv, slot):
        h, r = chains[c]
        s = _dot_nt(q_ref[0, h, r * rows:(r + 1) * rows, :], kv)
        m_prev = m_ref[c]
        m_new = jnp.maximum(m_prev, jnp.max(s, axis=1, keepdims=True))
        alpha_ref[slot, c] = jnp.exp2(m_prev - m_new)
        p_ref[slot, c] = jnp.exp2(s - jnp.concatenate([m_new] * (tk // LANES), axis=1)).astype(BF16)
        m_ref[c] = m_new

    def accumulate(c, kv, slot):
        alpha = alpha_ref[slot, c]
        acc_ref[c] = acc_ref[c] * jnp.concatenate([alpha, alpha], axis=1) + _dot(p_ref[slot, c], kv)

    def stage(j_acc, slot_acc, j_probs, slot_probs):
        kv_acc = None if j_acc is None else kv_block(j_acc)
        kv_probs = None if j_probs is None else kv_block(j_probs)
        for c in range(len(chains)):
            if j_acc is not None:
                accumulate(c, kv_acc, slot_acc)
            if j_probs is not None:
                probs(c, kv_probs, slot_probs)

    stage(None, None, 0, 0)
    n_pairs = (n_kv - 1) // 2

    def body(i, carry):
        j = 2 * i + 1
        stage(j - 1, 0, j, 1)
        stage(j, 1, j + 1, 0)
        return carry

    lax.fori_loop(0, n_pairs, body, 0)
    if (n_kv - 1) % 2:
        stage(n_kv - 2, 0, n_kv - 1, 1)
        stage(n_kv - 1, 1, None, None)
    else:
        stage(n_kv - 1, 0, None, None)

    h_ml = hf_ref[0].astype(F32) + hb_ref[0].astype(F32)
    ng = ng_ref[...]
    for hd in range(MLSTM_HEADS):
        sl = slice(hd * MLSTM_HEAD_DIM, (hd + 1) * MLSTM_HEAD_DIM)
        y_ref[:, sl] = (og_ref[0, :, sl].astype(F32) * _rms(h_ml[:, sl], ng[:, sl])).astype(BF16)
    for c, (h, r) in enumerate(chains):
        acc = acc_ref[c]
        o_lat = (acc[:, :KV_LORA] / acc[:, ATT_DIM - 1:ATT_DIM]).astype(BF16)
        lo = MLSTM_WIDTH + h * V_HEAD
        y_ref[r * rows:(r + 1) * rows, lo:lo + V_HEAD] = _dot(o_lat, wuv_ref[h]).astype(BF16)
    o_ref[0] = x_ref[0] + _dot(y_ref[...], wout_ref[...])


def _mla_attn_out(qa, ka, wuv, hf, hb, og, x, ng, wout, tq, tk):
    b, heads, s, _ = qa.shape
    d = x.shape[-1]
    rows = min(tq, ATT_CHAIN_ROWS)
    n_chains = heads * (tq // rows)
    row = lambda width: pl.BlockSpec((1, tq, width), lambda bi, qi: (bi, qi, 0))

    def resident(shape, index_map):
        return pl.BlockSpec(shape, index_map, pipeline_mode=pl.Buffered(1))

    return pl.pallas_call(
        functools.partial(_mla_attn_kernel, tk=tk, rows=rows),
        grid=(b, s // tq),
        in_specs=[
            pl.BlockSpec((1, heads, tq, ATT_DIM), lambda bi, qi: (bi, 0, qi, 0)),
            resident((1, s, ATT_DIM), lambda bi, qi: (bi, 0, 0)),
            resident(wuv.shape, lambda bi, qi: (0, 0, 0)),
            row(MLSTM_WIDTH), row(MLSTM_WIDTH), row(MLSTM_WIDTH), row(d),
            resident(ng.shape, lambda bi, qi: (0, 0)),
            resident(wout.shape, lambda bi, qi: (0, 0)),
        ],
        out_specs=row(d),
        out_shape=jax.ShapeDtypeStruct((b, s, d), F32),
        scratch_shapes=[pltpu.VMEM((n_chains, rows, LANES), F32), pltpu.VMEM((2, n_chains, rows, LANES), F32),
                        pltpu.VMEM((2, n_chains, rows, tk), BF16), pltpu.VMEM((n_chains, rows, ATT_DIM), F32),
                        pltpu.VMEM((tq, MLSTM_WIDTH + MLA_WIDTH), BF16)],
        compiler_params=pltpu.CompilerParams(
            dimension_semantics=("parallel", "parallel"), vmem_limit_bytes=VMEM_LIMIT_BYTES),
        name="mla_attn_out",
    )(qa, ka, wuv, hf, hb, og, x, ng, wout)


def _ffn_ple_kernel(h_ref, hp_ref, hn_ref, g_ref, wu_ref, cw_ref, cb_ref, wd_ref, p_ref, pg_ref,
                    wpg_ref, wpp_ref, ppg_ref, fg_ref, o_ref, a_ref, *, n_tiles, final_norm):
    tile = pl.program_id(1)
    d_ff = wd_ref.shape[0]
    h1 = h_ref[0]
    xa = _rms_rows_with_halo(h1, hp_ref[0], hn_ref[0], g_ref[...])

    for lo in range(0, d_ff, FF_CHUNK):
        g_sl = slice(lo, lo + FF_CHUNK)
        v_sl = slice(d_ff + lo, d_ff + lo + FF_CHUNK)
        gate = _conv3_rows(_dot(xa, wu_ref[:, g_sl]), cw_ref[:, g_sl], cb_ref[:, g_sl], tile, n_tiles)
        val = _conv3_rows(_dot(xa, wu_ref[:, v_sl]), cw_ref[:, v_sl], cb_ref[:, v_sl], tile, n_tiles)
        a_ref[:, g_sl] = (_silu(gate) * val).astype(BF16)

    h2 = h1 + _dot(a_ref[...], wd_ref[...])
    gate_p = 0.5 * jnp.tanh(0.5 * _dot(_rms(h2, pg_ref[...]).astype(BF16), wpg_ref[...])) + 0.5
    emb = _rms(_dot(p_ref[0].astype(BF16), wpp_ref[...]), ppg_ref[...])
    h3 = h2 + gate_p * emb
    o_ref[0] = _rms(h3, fg_ref[...]) if final_norm else h3


def _ffn_ple(h1, p, w, tm, final_norm):
    b, s, d = h1.shape
    d_ff = w["wd"].shape[0]
    n_tiles = s // tm
    hb = tm // HALO_ROWS
    n_halo = s // HALO_ROWS
    ple = p.shape[-1]

    def const(shape):
        return pl.BlockSpec(shape, lambda bi, ti: (0,) * len(shape), pipeline_mode=pl.Buffered(1))

    in_specs = [
        pl.BlockSpec((1, tm, d), lambda bi, ti: (bi, ti, 0)),
        pl.BlockSpec((1, HALO_ROWS, d), lambda bi, ti: (bi, jnp.maximum(ti * hb - 1, 0), 0)),
        pl.BlockSpec((1, HALO_ROWS, d), lambda bi, ti: (bi, jnp.minimum((ti + 1) * hb, n_halo - 1), 0)),
        const((1, d)), const((d, 2 * d_ff)), const((3, 2 * d_ff)), const((1, 2 * d_ff)), const((d_ff, d)),
        pl.BlockSpec((1, tm, ple), lambda bi, ti: (bi, ti, 0)),
        const((1, d)), const((d, d)), const((ple, d)), const((1, d)), const((1, d)),
    ]
    return pl.pallas_call(
        functools.partial(_ffn_ple_kernel, n_tiles=n_tiles, final_norm=final_norm),
        grid=(b, n_tiles),
        in_specs=in_specs,
        out_specs=pl.BlockSpec((1, tm, d), lambda bi, ti: (bi, ti, 0)),
        out_shape=jax.ShapeDtypeStruct((b, s, d), F32),
        scratch_shapes=[pltpu.VMEM((tm, d_ff), BF16)],
        compiler_params=pltpu.CompilerParams(
            dimension_semantics=("parallel", "parallel"), vmem_limit_bytes=VMEM_LIMIT_BYTES,
            allow_input_fusion=[i in (4, 7, 10, 11) for i in range(len(in_specs))]),
        name="ffn_ple",
    )(h1, h1, h1, w["g"], w["wu"], w["cw"], w["cb"], w["wd"], p,
      w["pg"], w["wpg"], w["wpp"], w["ppg"], w["fg"])


def _pick_tile(n, target):
    t = min(n, target)
    while n % t:
        t //= 2
    return t


def _rope_tables(s):
    pos = jnp.arange(s, dtype=F32)
    inv_freq = ROPE_THETA ** (-jnp.arange(0, QK_ROPE, 2, dtype=F32) / QK_ROPE)
    ang = pos[:, None] * inv_freq[None, :]
    cos, sin = jnp.cos(ang), jnp.sin(ang)
    pad = jnp.zeros((s, LANES - QK_ROPE), F32)
    return (jnp.concatenate([cos, cos, pad], axis=1), jnp.concatenate([-sin, sin, pad], axis=1))


def _swap_halves(w):
    half = w.shape[-1] // 2
    return jnp.concatenate([w[..., half:], w[..., :half]], axis=-1)


def _pad_lanes(w, width):
    return jnp.concatenate([w, jnp.zeros(w.shape[:-1] + (width - w.shape[-1],), w.dtype)], axis=-1)


def _in_proj_weights(w_in, b_gates, conv_w, conv_b, q_norm_g, w_uq, kv_norm_g, w_ukv, cos_t, sin_t):
    d = w_in.shape[0]
    o_q, o_v, o_o = 0, 2 * MLSTM_WIDTH, 3 * MLSTM_WIDTH
    o_g = 4 * MLSTM_WIDTH
    o_cq = o_g + N_GATES
    o_ckv = o_cq + Q_LORA
    o_kr = o_ckv + KV_LORA
    w_g = w_in[:, o_g:o_cq]
    w_kr = w_in[:, o_kr:o_kr + QK_ROPE]
    wc = jnp.concatenate([w_in[:, o_cq:o_kr], _pad_lanes(w_kr, LANES), _pad_lanes(_swap_halves(w_kr), LANES)],
                         axis=1)
    uq = w_uq.reshape(Q_LORA, MLA_HEADS, QK_NOPE + QK_ROPE)
    uq_rope = uq[:, :, QK_NOPE:]
    ukv = w_ukv.reshape(KV_LORA, MLA_HEADS, QK_NOPE + V_HEAD)
    return {
        "wqk": w_in[:, o_q:o_v].astype(BF16),
        "wv": w_in[:, o_v:o_o].astype(BF16),
        "wo": w_in[:, o_o:o_g].astype(BF16),
        "wgt": w_g.T.astype(BF16),
        "bgt": b_gates[:, None],
        "wc": wc.astype(BF16),
        "cw": conv_w, "cb": conv_b[None, :],
        "qg": q_norm_g[None, :],
        "wuqn": uq[:, :, :QK_NOPE].reshape(Q_LORA, MLA_HEADS * QK_NOPE).astype(BF16),
        "wuqa": _pad_lanes(uq_rope, LANES).reshape(Q_LORA, MLA_HEADS * LANES).astype(BF16),
        "wuqb": _pad_lanes(_swap_halves(uq_rope), LANES).reshape(Q_LORA, MLA_HEADS * LANES).astype(BF16),
        "kvg": kv_norm_g[None, :],
        "wukt": jnp.transpose(ukv[:, :, :QK_NOPE], (1, 2, 0)).astype(BF16),
        "cos": cos_t, "sin": sin_t,
    }, jnp.transpose(ukv[:, :, QK_NOPE:], (1, 0, 2)).astype(BF16)


def kernel(x, p, ln_mix_g, w_in, b_gates, conv_qk_w, conv_qk_b, mlstm_norm_g, q_norm_g, w_uq, kv_norm_g, w_ukv, w_out, ln_ffn_g, w_up, conv_ffn_w, conv_ffn_b, w_down, ple_norm_g, w_ple_gate, w_ple_proj, ple_post_g, final_g):
    depth = w_in.shape[0]
    _, s, _ = x.shape
    assert w_down.shape[1] % FF_CHUNK == 0
    tm = _pick_tile(s, 512)
    tq = _pick_tile(s, 512)
    tk = _pick_tile(s, 1024)
    cos_t, sin_t = _rope_tables(s)
    h = x
    for l in range(depth):
        w1, wuv = _in_proj_weights(w_in[l], b_gates[l], conv_qk_w[l], conv_qk_b[l], q_norm_g[l], w_uq[l],
                                   kv_norm_g[l], w_ukv[l], cos_t, sin_t)
        qm, kt, vm, og, gt, qa, ka = _in_proj(h, ln_mix_g[l][None, :], w1, _pick_tile(s, 1024))
        hf, hb = _mlstm(qm, kt, vm, gt)
        h1 = _mla_attn_out(qa, ka, wuv, hf, hb, og, h, mlstm_norm_g[l][None, :],
                           w_out[l].astype(BF16), tq, tk)
        w5 = {
            "g": ln_ffn_g[l][None, :], "wu": w_up[l].astype(BF16), "cw": conv_ffn_w[l],
            "cb": conv_ffn_b[l][None, :], "wd": w_down[l].astype(BF16),
            "pg": ple_norm_g[l][None, :], "wpg": w_ple_gate[l].astype(BF16),
            "wpp": w_ple_proj[l].astype(BF16), "ppg": ple_post_g[l][None, :], "fg": final_g[None, :],
        }
        h = _ffn_ple(h1, p[l], w5, tm, final_norm=(l == depth - 1))
    return h
```
